```python
import math
import jax, jax.numpy as jnp
from jax import lax
import numpy as np

D_MODEL = 4096
BATCH = 8
SEQ = 2048
DEPTH = 4

CHUNK = 64
N_A_LAYERS = DEPTH // 2
N_B_LAYERS = DEPTH - N_A_LAYERS
HEAD_DIM = 64
N_HEADS = D_MODEL // HEAD_DIM
N_KV_HEADS = 8
KV_GROUP = N_HEADS // N_KV_HEADS
KV_WIDTH = N_KV_HEADS * HEAD_DIM
WINDOW = 128
WIN_CHUNKS = WINDOW // CHUNK
B_PREV_CHUNKS = 8
N_BUCKETS = 32
T5_MAX_DIST = 128
REL_CLIP = 128
N_GROUPS = 4
N_EXPERTS_PER_GROUP = 8
EXPERT_FF = 384
TOP_K = 2
PLE_DIM = 256
RMS_EPS = 1e-6
NEG_INF = -1e30

kernel_name = "yoco_chunk_swa_sink_relbias_hmoe_ple"


def rms_norm(x, g):
    xf = x.astype(jnp.float32)
    xf = xf * lax.rsqrt(jnp.mean(xf * xf, axis=-1, keepdims=True) + RMS_EPS)
    return (xf * g.astype(jnp.float32)).astype(x.dtype)


def t5_bucket(rel):
    nb = N_BUCKETS // 2
    n = -rel
    ret = np.where(n < 0, nb, 0)
    n = np.abs(n)
    max_exact = nb // 2
    large = max_exact + (np.log(np.maximum(n, 1) / max_exact)
                         / math.log(T5_MAX_DIST / max_exact) * (nb - max_exact)).astype(np.int32)
    large = np.minimum(large, nb - 1)
    return (ret + np.where(n < max_exact, n, large)).astype(np.int32)


def band_rel(n_prev):
    qi = np.arange(CHUNK)[:, None]
    kj = np.arange((n_prev + 1) * CHUNK)[None, :]
    return kj - n_prev * CHUNK - qi


def chunk_band_attention(q, k, v, bias, n_prev, sinks):
    b, s = q.shape[0], q.shape[1]
    nc = s // CHUNK
    band = (n_prev + 1) * CHUNK
    pad = n_prev * CHUNK
    kp = jnp.pad(k, ((0, 0), (pad, 0), (0, 0), (0, 0)))
    vp = jnp.pad(v, ((0, 0), (pad, 0), (0, 0), (0, 0)))
    qc = q.reshape(b, nc, CHUNK, N_KV_HEADS, KV_GROUP, HEAD_DIM)
    qc = jnp.moveaxis(qc, 1, 0)
    bias_g = bias.astype(jnp.float32).reshape(N_KV_HEADS, KV_GROUP, CHUNK, band)
    key_slot = jnp.arange(band)
    scale = HEAD_DIM ** -0.5

    def one_chunk(args):
        c, qb = args
        kb = lax.dynamic_slice_in_dim(kp, c * CHUNK, band, axis=1)
        vb = lax.dynamic_slice_in_dim(vp, c * CHUNK, band, axis=1)
        sc = jnp.einsum('bqhgd,bkhd->bhgqk', qb, kb,
                        preferred_element_type=jnp.float32) * scale + bias_g
        valid = key_slot >= (n_prev - c) * CHUNK
        sc = jnp.where(valid, sc, NEG_INF)
        if sinks is not None:
            sink = jnp.broadcast_to(
                sinks.astype(jnp.float32).reshape(N_KV_HEADS, KV_GROUP, 1, 1),
                sc.shape[:-1] + (1,))
            pr = jax.nn.softmax(jnp.concatenate([sc, sink], axis=-1), axis=-1)[..., :-1]
        else:
            pr = jax.nn.softmax(sc, axis=-1)
        return jnp.einsum('bhgqk,bkhd->bqhgd', pr.astype(vb.dtype), vb)

    o = lax.map(one_chunk, (jnp.arange(nc), qc))
    return jnp.moveaxis(o, 0, 1).reshape(b, s, N_HEADS * HEAD_DIM)


def mixer_a(x, norm_g, w_qkv, q_g, k_g, sinks, w_o, t5_bias):
    b, s, _ = x.shape
    h = rms_norm(x, norm_g)
    qkv = h @ w_qkv
    q = qkv[..., :D_MODEL].reshape(b, s, N_HEADS, HEAD_DIM)
    k = qkv[..., D_MODEL:D_MODEL + KV_WIDTH].reshape(b, s, N_KV_HEADS, HEAD_DIM)
    v = qkv[..., D_MODEL + KV_WIDTH:].reshape(b, s, N_KV_HEADS, HEAD_DIM)
    q = rms_norm(q, q_g)
    k = rms_norm(k, k_g)
    bias = t5_bias[:, t5_bucket(band_rel(WIN_CHUNKS))]
    o = chunk_band_attention(q, k, v, bias, WIN_CHUNKS, sinks)
    return o @ w_o


def shared_kv(x, norm_g, w_kv, k_g):
    b, s, _ = x.shape
    h = rms_norm(x, norm_g)
    kv = h @ w_kv
    k = rms_norm(kv[..., :KV_WIDTH].reshape(b, s, N_KV_HEADS, HEAD_DIM), k_g)
    v = kv[..., KV_WIDTH:].reshape(b, s, N_KV_HEADS, HEAD_DIM)
    return k, v


def mixer_b(x, k, v, norm_g, w_q, q_g, rel_table, w_o):
    b, s, _ = x.shape
    h = rms_norm(x, norm_g)
    q = rms_norm((h @ w_q).reshape(b, s, N_HEADS, HEAD_DIM), q_g)
    dist = -band_rel(B_PREV_CHUNKS)
    idx = np.clip(dist, -REL_CLIP, REL_CLIP) + REL_CLIP
    bias = rel_table[:, idx]
    o = chunk_band_attention(q, k, v, bias, B_PREV_CHUNKS, None)
    return o @ w_o


def hier_moe(h, w_rg, w_re, w_gate, w_up, w_down):
    b, s, d = h.shape
    ht = h.reshape(b * s, d)
    g_logits = jnp.einsum('td,dg->tg', ht, w_rg, preferred_element_type=jnp.float32)
    g_prob = jax.nn.softmax(g_logits, axis=-1)
    g_sel = jnp.argmax(g_logits, axis=-1)
    g_w = jnp.take_along_axis(g_prob, g_sel[:, None], axis=-1)
    e_logits_all = jnp.einsum('td,gde->tge', ht, w_re, preferred_element_type=jnp.float32)
    e_logits = jnp.take_along_axis(e_logits_all, g_sel[:, None, None], axis=1)[:, 0]
    top_v, top_i = lax.top_k(e_logits, TOP_K)
    top_w = jax.nn.softmax(top_v, axis=-1)
    e_w = jnp.einsum('tk,tke->te', top_w,
                     jax.nn.one_hot(top_i, N_EXPERTS_PER_GROUP, dtype=jnp.float32))
    combine = (jax.nn.one_hot(g_sel, N_GROUPS, dtype=jnp.float32)[:, :, None]
               * g_w[:, :, None] * e_w[:, None, :])
    y = jnp.zeros_like(ht)
    for gi in range(N_GROUPS):
        a = jnp.einsum('td,edf->tef', ht, w_gate[gi])
        u = jnp.einsum('td,edf->tef', ht, w_up[gi])
        act = jax.nn.silu(a) * u * combine[:, gi, :, None].astype(ht.dtype)
        y = y + jnp.einsum('tef,efd->td', act, w_down[gi])
    return y.reshape(b, s, d)


def setup_inputs(seed: int = 0) -> dict:
    key = jax.random.key(seed)
    ks = jax.random.split(key, 32)
    f32 = jnp.float32
    D = D_MODEL
    G, E, F = N_GROUPS, N_EXPERTS_PER_GROUP, EXPERT_FF

    def nrm(k, shape, scale):
        return jax.random.normal(k, shape, f32) * scale

    def gain(k, shape):
        return 1.0 + 0.05 * jax.random.normal(k, shape, f32)

    return {
        "x": nrm(ks[0], (BATCH, SEQ, D), 1.0),
        "p": nrm(ks[1], (DEPTH, BATCH, SEQ, PLE_DIM), 1.0),
        "t5_bias": nrm(ks[2], (N_HEADS, N_BUCKETS), 0.5),
        "attn_norm_a": gain(ks[3], (N_A_LAYERS, D)),
        "w_qkv_a": nrm(ks[4], (N_A_LAYERS, D, D + 2 * KV_WIDTH), D ** -0.5),
        "q_norm_a": gain(ks[5], (N_A_LAYERS, HEAD_DIM)),
        "k_norm_a": gain(ks[6], (N_A_LAYERS, HEAD_DIM)),
        "sinks_a": nrm(ks[7], (N_A_LAYERS, N_HEADS), 1.0),
        "w_o_a": nrm(ks[8], (N_A_LAYERS, D, D), D ** -0.5),
        "kv_norm_b": gain(ks[9], (D,)),
        "w_kv_b": nrm(ks[10], (D, 2 * KV_WIDTH), D ** -0.5),
        "k_norm_b": gain(ks[11], (HEAD_DIM,)),
        "attn_norm_b": gain(ks[12], (N_B_LAYERS, D)),
        "w_q_b": nrm(ks[13], (N_B_LAYERS, D, D), D ** -0.5),
        "q_norm_b": gain(ks[14], (N_B_LAYERS, HEAD_DIM)),
        "rel_bias_b": nrm(ks[15], (N_B_LAYERS, N_HEADS, 2 * REL_CLIP + 1), 0.5),
        "w_o_b": nrm(ks[16], (N_B_LAYERS, D, D), D ** -0.5),
        "ffn_norm": gain(ks[17], (DEPTH, D)),
        "w_router_group": nrm(ks[18], (DEPTH, D, G), D ** -0.5),
        "w_router_expert": nrm(ks[19], (DEPTH, G, D, E), D ** -0.5),
        "w_exp_gate": nrm(ks[20], (DEPTH, G, E, D, F), D ** -0.5),
        "w_exp_up": nrm(ks[21], (DEPTH, G, E, D, F), D ** -0.5),
        "w_exp_down": nrm(ks[22], (DEPTH, G, E, F, D), F ** -0.5),
        "w_ple_proj": nrm(ks[23], (DEPTH, PLE_DIM, D), PLE_DIM ** -0.5),
        "ple_norm": gain(ks[24], (DEPTH, D)),
        "w_ple_gate_down": nrm(ks[25], (DEPTH, D, PLE_DIM), D ** -0.5),
        "w_ple_gate_up": nrm(ks[26], (DEPTH, PLE_DIM, D), PLE_DIM ** -0.5),
    }


def reference(x, p, t5_bias, attn_norm_a, w_qkv_a, q_norm_a, k_norm_a, sinks_a, w_o_a,
              kv_norm_b, w_kv_b, k_norm_b, attn_norm_b, w_q_b, q_norm_b, rel_bias_b, w_o_b,
              ffn_norm, w_router_group, w_router_expert, w_exp_gate, w_exp_up, w_exp_down,
              w_ple_proj, ple_norm, w_ple_gate_down, w_ple_gate_up):
    k_b, v_b = None, None
    for i in range(DEPTH):
        if i < N_A_LAYERS:
            x = x + mixer_a(x, attn_norm_a[i], w_qkv_a[i], q_norm_a[i], k_norm_a[i],
                            sinks_a[i], w_o_a[i], t5_bias)
        else:
            j = i - N_A_LAYERS
            x = x + mixer_b(x, k_b, v_b, attn_norm_b[j], w_q_b[j], q_norm_b[j],
                            rel_bias_b[j], w_o_b[j])
        x = x + hier_moe(rms_norm(x, ffn_norm[i]), w_router_group[i], w_router_expert[i],
                         w_exp_gate[i], w_exp_up[i], w_exp_down[i])
        gate = jax.nn.sigmoid((rms_norm(x, ple_norm[i]) @ w_ple_gate_down[i]) @ w_ple_gate_up[i])
        x = x + gate * (p[i] @ w_ple_proj[i])
        if i == N_A_LAYERS - 1:
            k_b, v_b = shared_kv(x, kv_norm_b, w_kv_b, k_norm_b)
    return x
```

```python
import functools

import numpy as np
import jax
import jax.numpy as jnp
from jax import lax
from jax.experimental import pallas as pl
from jax.experimental.pallas import tpu as pltpu

D_MODEL = 4096
BATCH = 8
SEQ = 2048
DEPTH = 4
CHUNK = 64
N_A_LAYERS = DEPTH // 2
HEAD_DIM = 64
N_HEADS = D_MODEL // HEAD_DIM
N_KV_HEADS = 8
KV_GROUP = N_HEADS // N_KV_HEADS
KV_WIDTH = N_KV_HEADS * HEAD_DIM
WIN_CHUNKS = 2
B_PREV_CHUNKS = 8
N_BUCKETS = 32
T5_MAX_DIST = 128
REL_CLIP = 128
N_GROUPS = 4
N_EXPERTS_PER_GROUP = 8
N_EXPERTS = N_GROUPS * N_EXPERTS_PER_GROUP
EXPERT_FF = 384
TOP_K = 2
PLE_DIM = 256
RMS_EPS = 1e-6
NEG_INF = -1e30

LANES = 128
HEAD_TILE = KV_GROUP * HEAD_DIM
ROUTER_LANES = LANES
VMEM_LIMIT = 56 * 1024 * 1024

TM_PROJ = 512
TM_PLE = 256
TM_EXPERT = 256

F32 = jnp.float32
BF16 = jnp.bfloat16


def _params(*sem):
    return pltpu.CompilerParams(dimension_semantics=sem, vmem_limit_bytes=VMEM_LIMIT)


def _rms(x, g):
    return x * lax.rsqrt(jnp.mean(x * x, axis=-1, keepdims=True) + RMS_EPS) * g


def _norm_proj_kernel(x_ref, g_ref, w_ref, cg_ref, bd_ref, o_ref, h_ref, *, n_norm_tiles):
    j = pl.program_id(1)

    @pl.when(j == 0)
    def _():
        h_ref[...] = _rms(x_ref[...], g_ref[...]).astype(BF16)

    acc = jnp.dot(h_ref[...], w_ref[...], preferred_element_type=F32)

    @pl.when(j < n_norm_tiles)
    def _():
        ss = jnp.dot((acc * acc).astype(BF16), bd_ref[...], preferred_element_type=F32)
        o_ref[0] = (acc * lax.rsqrt(ss * (1.0 / HEAD_DIM) + RMS_EPS) * cg_ref[...]).astype(o_ref.dtype)

    @pl.when(j >= n_norm_tiles)
    def _():
        o_ref[0] = acc.astype(o_ref.dtype)


def _norm_proj(x, g, w, col_gain, n_norm_tiles):
    t, d = x.shape
    n = w.shape[1]
    tn = HEAD_TILE
    bd = jnp.asarray(np.kron(np.eye(tn // HEAD_DIM), np.ones((HEAD_DIM, HEAD_DIM))), BF16)
    return pl.pallas_call(
        functools.partial(_norm_proj_kernel, n_norm_tiles=n_norm_tiles),
        grid=(t // TM_PROJ, n // tn),
        in_specs=[
            pl.BlockSpec((TM_PROJ, d), lambda i, j: (i, 0)),
            pl.BlockSpec((1, d), lambda i, j: (0, 0)),
            pl.BlockSpec((d, tn), lambda i, j: (0, j)),
            pl.BlockSpec((1, tn), lambda i, j: (0, j)),
            pl.BlockSpec((tn, tn), lambda i, j: (0, 0)),
        ],
        out_specs=pl.BlockSpec((1, TM_PROJ, tn), lambda i, j: (j, i, 0)),
        out_shape=jax.ShapeDtypeStruct((n // tn, t, tn), BF16),
        scratch_shapes=[pltpu.VMEM((TM_PROJ, d), BF16)],
        compiler_params=_params("parallel", "arbitrary"),
        name="norm_proj",
    )(x, g.reshape(1, d), w, col_gain.reshape(1, n), bd)


def _attn_kernel(q_ref, k_ref, v_ref, bias_ref, sink_ref, o_ref, *, n_prev, has_sink):
    c = pl.program_id(1)
    band = (n_prev + 1) * CHUNK
    row0 = pl.multiple_of(c * CHUNK, CHUNK)
    kslot = lax.broadcasted_iota(jnp.int32, (1, band), 1)
    valid = kslot >= (n_prev - c) * CHUNK
    lo = lax.broadcasted_iota(jnp.int32, (CHUNK, LANES), 1) < HEAD_DIM
    pairs = KV_GROUP // 2

    def one_kv_head(h, carry):
        qh = q_ref[h]
        zero = jnp.zeros((CHUNK, LANES), qh.dtype)
        parts = []
        for gg in range(pairs):
            qp = qh[:, gg * LANES:(gg + 1) * LANES]
            parts.append(jnp.where(lo, qp, zero))
            parts.append(jnp.where(lo, zero, qp))
        lhs = jnp.concatenate(parts, axis=0)
        k2 = k_ref[0, h, pl.ds(row0, band), :]
        v2 = v_ref[0, h, pl.ds(row0, band), :]
        s = lax.dot_general(lhs, k2, (((1,), (1,)), ((), ())), preferred_element_type=F32)
        s = s + bias_ref[pl.ds(h * KV_GROUP, KV_GROUP)].reshape(KV_GROUP * CHUNK, band)
        s = jnp.where(valid, s, NEG_INF)
        m = jnp.max(s, axis=-1, keepdims=True)
        if has_sink:
            sk = sink_ref[h]
            m = jnp.maximum(m, sk)
        p = jnp.exp(s - m)
        l = jnp.sum(p, axis=-1, keepdims=True)
        if has_sink:
            l = l + jnp.exp(sk - m)
        o2 = jnp.dot(p.astype(BF16), v2, preferred_element_type=F32) * (1.0 / l)
        outs = []
        for gg in range(pairs):
            a = o2[(2 * gg) * CHUNK:(2 * gg + 1) * CHUNK]
            b = o2[(2 * gg + 1) * CHUNK:(2 * gg + 2) * CHUNK]
            outs.append(jnp.where(lo, a, b))
        o_ref[h] = jnp.concatenate(outs, axis=1).astype(o_ref.dtype)
        return carry

    lax.fori_loop(0, N_KV_HEADS, one_kv_head, 0)


def _attention(q3, k2, v2, bias, sinks, n_prev):
    t = q3.shape[1]
    nc = SEQ // CHUNK
    band = (n_prev + 1) * CHUNK
    spad = k2.shape[2]
    has_sink = sinks is not None
    if has_sink:
        sink_rows = jnp.broadcast_to(sinks.astype(F32).reshape(N_KV_HEADS, KV_GROUP, 1, 1),
                                     (N_KV_HEADS, KV_GROUP, CHUNK, 1)).reshape(N_KV_HEADS, KV_GROUP * CHUNK, 1)
    else:
        sink_rows = jnp.zeros((N_KV_HEADS, KV_GROUP * CHUNK, 1), F32)
    return pl.pallas_call(
        functools.partial(_attn_kernel, n_prev=n_prev, has_sink=has_sink),
        grid=(BATCH, nc),
        in_specs=[
            pl.BlockSpec((N_KV_HEADS, CHUNK, HEAD_TILE), lambda b, c: (0, b * nc + c, 0)),
            pl.BlockSpec((1, N_KV_HEADS, spad, LANES), lambda b, c: (b, 0, 0, 0)),
            pl.BlockSpec((1, N_KV_HEADS, spad, LANES), lambda b, c: (b, 0, 0, 0)),
            pl.BlockSpec((N_HEADS, CHUNK, band), lambda b, c: (0, 0, 0)),
            pl.BlockSpec((N_KV_HEADS, KV_GROUP * CHUNK, 1), lambda b, c: (0, 0, 0)),
        ],
        out_specs=pl.BlockSpec((N_KV_HEADS, CHUNK, HEAD_TILE), lambda b, c: (0, b * nc + c, 0)),
        out_shape=jax.ShapeDtypeStruct((N_KV_HEADS, t, HEAD_TILE), BF16),
        compiler_params=_params("parallel", "arbitrary"),
        name="band_attention",
    )(q3, k2, v2, bias, sink_rows)


def _band_kv(kv_tile, n_prev):
    x = kv_tile.reshape(BATCH, SEQ, N_KV_HEADS, HEAD_DIM).transpose(0, 2, 1, 3)
    x = jnp.concatenate([x, x], axis=-1)
    return jnp.pad(x, ((0, 0), (0, 0), (n_prev * CHUNK, 0), (0, 0)))


def _out_proj_kernel(a_ref, w_ref, r_ref, o_ref):
    acc = r_ref[...]
    for h in range(N_KV_HEADS):
        acc = acc + jnp.dot(a_ref[h], w_ref[h], preferred_element_type=F32)
    o_ref[...] = acc


def _out_proj(a3, w, res):
    t, d = res.shape
    tn = HEAD_TILE
    w3 = w.reshape(N_KV_HEADS, HEAD_TILE, d)
    return pl.pallas_call(
        _out_proj_kernel,
        grid=(t // TM_PROJ, d // tn),
        in_specs=[
            pl.BlockSpec((N_KV_HEADS, TM_PROJ, HEAD_TILE), lambda i, j: (0, i, 0)),
            pl.BlockSpec((N_KV_HEADS, HEAD_TILE, tn), lambda i, j: (0, 0, j)),
            pl.BlockSpec((TM_PROJ, tn), lambda i, j: (i, j)),
        ],
        out_specs=pl.BlockSpec((TM_PROJ, tn), lambda i, j: (i, j)),
        out_shape=jax.ShapeDtypeStruct((t, d), F32),
        compiler_params=_params("parallel", "arbitrary"),
        name="out_proj",
    )(a3, w3, res)


def _router_kernel(x_ref, g_ref, whi_ref, wlo_ref, h_ref, route_ref):
    hf = _rms(x_ref[...], g_ref[...])
    hb = hf.astype(BF16)
    h_ref[...] = hb
    hlo = (hf - hb.astype(F32)).astype(BF16)
    whi = whi_ref[...]
    logits = (jnp.dot(hb, whi, preferred_element_type=F32)
              + jnp.dot(hb, wlo_ref[...], preferred_element_type=F32)
              + jnp.dot(hlo, whi, preferred_element_type=F32))
    lane = lax.broadcasted_iota(jnp.int32, logits.shape, 1).astype(F32)
    big = float(ROUTER_LANES)
    is_g = lane < N_GROUPS
    gl = jnp.where(is_g, logits, NEG_INF)
    gmax = jnp.max(gl, axis=-1, keepdims=True)
    gidx = jnp.min(jnp.where(gl == gmax, lane, big), axis=-1, keepdims=True)
    gsum = jnp.sum(jnp.where(is_g, jnp.exp(gl - gmax), 0.0), axis=-1, keepdims=True)
    g_w = 1.0 / gsum
    e0 = N_GROUPS + gidx * N_EXPERTS_PER_GROUP
    el = jnp.where((lane >= e0) & (lane < e0 + N_EXPERTS_PER_GROUP), logits, NEG_INF)
    m1 = jnp.max(el, axis=-1, keepdims=True)
    i1 = jnp.min(jnp.where(el == m1, lane, big), axis=-1, keepdims=True)
    el2 = jnp.where(lane == i1, NEG_INF, el)
    m2 = jnp.max(el2, axis=-1, keepdims=True)
    i2 = jnp.min(jnp.where(el2 == m2, lane, big), axis=-1, keepdims=True)
    ex = jnp.exp(m2 - m1)
    w1 = 1.0 / (1.0 + ex)
    w2 = ex * w1
    route = jnp.where(lane == 0, i1 - N_GROUPS,
                      jnp.where(lane == 1, i2 - N_GROUPS,
                                jnp.where(lane == 2, g_w * w1,
                                          jnp.where(lane == 3, g_w * w2, 0.0))))
    route_ref[...] = route


def _router(x, g, w_rg, w_re):
    t, d = x.shape
    w = jnp.concatenate([w_rg, w_re.transpose(1, 0, 2).reshape(d, N_EXPERTS)], axis=1)
    w = jnp.pad(w, ((0, 0), (0, ROUTER_LANES - w.shape[1])))
    whi = w.astype(BF16)
    wlo = (w - whi.astype(F32)).astype(BF16)
    return pl.pallas_call(
        _router_kernel,
        grid=(t // TM_PROJ,),
        in_specs=[
            pl.BlockSpec((TM_PROJ, d), lambda i: (i, 0)),
            pl.BlockSpec((1, d), lambda i: (0, 0)),
            pl.BlockSpec((d, ROUTER_LANES), lambda i: (0, 0)),
            pl.BlockSpec((d, ROUTER_LANES), lambda i: (0, 0)),
        ],
        out_specs=[
            pl.BlockSpec((TM_PROJ, d), lambda i: (i, 0)),
            pl.BlockSpec((TM_PROJ, ROUTER_LANES), lambda i: (i, 0)),
        ],
        out_shape=[jax.ShapeDtypeStruct((t, d), BF16), jax.ShapeDtypeStruct((t, ROUTER_LANES), F32)],
        compiler_params=_params("parallel"),
        name="router",
    )(x, g.reshape(1, d), whi, wlo)


def _experts_kernel(te_ref, nu_ref, xs_ref, cw_ref, wg_ref, wu_ref, wd_ref, y_ref):
    i = pl.program_id(0)

    @pl.when(i < nu_ref[0])
    def _():
        xs = xs_ref[...]
        a = jnp.dot(xs, wg_ref[0], preferred_element_type=F32)
        u = jnp.dot(xs, wu_ref[0], preferred_element_type=F32)
        act = a * jax.nn.sigmoid(a) * u * cw_ref[...]
        y_ref[...] = jnp.dot(act.astype(BF16), wd_ref[0], preferred_element_type=F32).astype(y_ref.dtype)

    @pl.when(i >= nu_ref[0])
    def _():
        y_ref[...] = jnp.zeros_like(y_ref)


def _experts(xs, cw, tile_expert, n_used, wg, wu, wd):
    np_rows, d = xs.shape
    f = wg.shape[2]
    n_tiles = np_rows // TM_EXPERT
    last = lambda i, te, nu: jnp.minimum(i, nu[0] - 1)
    grid_spec = pltpu.PrefetchScalarGridSpec(
        num_scalar_prefetch=2,
        grid=(n_tiles,),
        in_specs=[
            pl.BlockSpec((TM_EXPERT, d), lambda i, te, nu: (last(i, te, nu), 0)),
            pl.BlockSpec((TM_EXPERT, 1), lambda i, te, nu: (last(i, te, nu), 0)),
            pl.BlockSpec((1, d, f), lambda i, te, nu: (te[i], 0, 0)),
            pl.BlockSpec((1, d, f), lambda i, te, nu: (te[i], 0, 0)),
            pl.BlockSpec((1, f, d), lambda i, te, nu: (te[i], 0, 0)),
        ],
        out_specs=pl.BlockSpec((TM_EXPERT, d), lambda i, te, nu: (i, 0)),
    )
    return pl.pallas_call(
        _experts_kernel,
        grid_spec=grid_spec,
        out_shape=jax.ShapeDtypeStruct((np_rows, d), BF16),
        compiler_params=_params("arbitrary"),
        name="experts",
    )(tile_expert, n_used, xs, cw, wg, wu, wd)


def _dispatch_plan(route):
    t = route.shape[0]
    n_assign = t * TOP_K
    np_rows = n_assign + N_EXPERTS * TM_EXPERT
    n_tiles = np_rows // TM_EXPERT
    ef = route[:, :TOP_K].astype(jnp.int32).reshape(-1)
    wf = route[:, TOP_K:2 * TOP_K].reshape(-1)
    order = jnp.argsort(ef, stable=True).astype(jnp.int32)
    inv = jnp.argsort(order).astype(jnp.int32)
    counts = jnp.sum((ef[:, None] == jnp.arange(N_EXPERTS)[None, :]).astype(jnp.int32), axis=0)
    padded = (counts + TM_EXPERT - 1) // TM_EXPERT * TM_EXPERT
    pend = jnp.cumsum(padded)
    pstart = pend - padded
    start = jnp.cumsum(counts) - counts
    pos = pstart[ef] + inv - start[ef]
    n_used = (pend[-1] // TM_EXPERT).astype(jnp.int32)
    tile_ids = jnp.arange(n_tiles, dtype=jnp.int32)
    te = jnp.sum((pend[None, :] // TM_EXPERT <= tile_ids[:, None]).astype(jnp.int32), axis=1)
    te = jnp.minimum(te, N_EXPERTS - 1)
    te_last = te[jnp.maximum(n_used - 1, 0)]
    te = jnp.where(tile_ids < n_used, te, te_last).astype(jnp.int32)
    rows = jnp.arange(np_rows, dtype=jnp.int32)
    e_r = te[rows // TM_EXPERT]
    k_r = rows - pstart[e_r]
    ok = (k_r < counts[e_r]) & (rows < pend[-1])
    a_r = order[jnp.clip(start[e_r] + k_r, 0, n_assign - 1)]
    src_token = jnp.where(ok, a_r // TOP_K, 0)
    cw = jnp.where(ok, wf[a_r], 0.0).reshape(np_rows, 1)
    return src_token, cw, te, n_used.reshape(1), pos


def _ple_kernel(x_ref, yy_ref, p_ref, g_ref, wgd_ref, wgu_ref, wp_ref, o_ref):
    d = x_ref.shape[1]
    x = x_ref[...] + yy_ref[:, :d].astype(F32) + yy_ref[:, d:].astype(F32)
    hn = _rms(x, g_ref[...]).astype(BF16)
    tdown = jnp.dot(hn, wgd_ref[...], preferred_element_type=F32)
    gate = jax.nn.sigmoid(jnp.dot(tdown.astype(BF16), wgu_ref[...], preferred_element_type=F32))
    pp = jnp.dot(p_ref[...].astype(BF16), wp_ref[...], preferred_element_type=F32)
    o_ref[...] = x + gate * pp


def _ple(x, yy, p, g, wgd, wgu, wp):
    t, d = x.shape
    return pl.pallas_call(
        _ple_kernel,
        grid=(t // TM_PLE,),
        in_specs=[
            pl.BlockSpec((TM_PLE, d), lambda i: (i, 0)),
            pl.BlockSpec((TM_PLE, TOP_K * d), lambda i: (i, 0)),
            pl.BlockSpec((TM_PLE, PLE_DIM), lambda i: (i, 0)),
            pl.BlockSpec((1, d), lambda i: (0, 0)),
            pl.BlockSpec((d, PLE_DIM), lambda i: (0, 0)),
            pl.BlockSpec((PLE_DIM, d), lambda i: (0, 0)),
            pl.BlockSpec((PLE_DIM, d), lambda i: (0, 0)),
        ],
        out_specs=pl.BlockSpec((TM_PLE, d), lambda i: (i, 0)),
        out_shape=jax.ShapeDtypeStruct((t, d), F32),
        compiler_params=_params("parallel"),
        name="ple",
    )(x, yy, p, g.reshape(1, d), wgd, wgu, wp)


def _band_rel(n_prev):
    qi = np.arange(CHUNK)[:, None]
    kj = np.arange((n_prev + 1) * CHUNK)[None, :]
    return kj - n_prev * CHUNK - qi


def _t5_bucket(rel):
    nb = N_BUCKETS // 2
    n = -rel
    ret = np.where(n < 0, nb, 0)
    n = np.abs(n)
    max_exact = nb // 2
    large = max_exact + (np.log(np.maximum(n, 1) / max_exact)
                         / np.log(T5_MAX_DIST / max_exact) * (nb - max_exact)).astype(np.int32)
    large = np.minimum(large, nb - 1)
    return (ret + np.where(n < max_exact, n, large)).astype(np.int32)


def _head_gain(g, n_heads, scale=1.0):
    return jnp.tile(g.astype(F32) * scale, n_heads)


def kernel(x, p, t5_bias, attn_norm_a, w_qkv_a, q_norm_a, k_norm_a, sinks_a, w_o_a, kv_norm_b, w_kv_b, k_norm_b, attn_norm_b, w_q_b, q_norm_b, rel_bias_b, w_o_b, ffn_norm, w_router_group, w_router_expert, w_exp_gate, w_exp_up, w_exp_down, w_ple_proj, ple_norm, w_ple_gate_down, w_ple_gate_up):
    b, s, d = x.shape
    t = b * s
    x = x.reshape(t, d)
    scale = HEAD_DIM ** -0.5
    bias_a = t5_bias.astype(F32)[:, _t5_bucket(_band_rel(WIN_CHUNKS))]
    rel_idx = np.clip(-_band_rel(B_PREV_CHUNKS), -REL_CLIP, REL_CLIP) + REL_CLIP
    ones_kv = jnp.ones((KV_WIDTH,), F32)
    k2_b = v2_b = None
    for i in range(DEPTH):
        if i < N_A_LAYERS:
            gains = jnp.concatenate([_head_gain(q_norm_a[i], N_HEADS, scale),
                                     _head_gain(k_norm_a[i], N_KV_HEADS), ones_kv])
            qkv3 = _norm_proj(x, attn_norm_a[i], w_qkv_a[i].astype(BF16), gains, N_KV_HEADS + 1)
            k2 = _band_kv(qkv3[N_KV_HEADS], WIN_CHUNKS)
            v2 = _band_kv(qkv3[N_KV_HEADS + 1], WIN_CHUNKS)
            o3 = _attention(qkv3, k2, v2, bias_a, sinks_a[i], WIN_CHUNKS)
            x = _out_proj(o3, w_o_a[i].astype(BF16), x)
        else:
            j = i - N_A_LAYERS
            q3 = _norm_proj(x, attn_norm_b[j], w_q_b[j].astype(BF16),
                            _head_gain(q_norm_b[j], N_HEADS, scale), N_KV_HEADS)
            bias_b = rel_bias_b[j].astype(F32)[:, rel_idx]
            o3 = _attention(q3, k2_b, v2_b, bias_b, None, B_PREV_CHUNKS)
            x = _out_proj(o3, w_o_b[j].astype(BF16), x)

        h, route = _router(x, ffn_norm[i], w_router_group[i], w_router_expert[i])
        src_token, cw, tile_expert, n_used, pos = _dispatch_plan(route)
        xs = jnp.take(h, src_token, axis=0)
        f = w_exp_gate.shape[-1]
        y = _experts(xs, cw, tile_expert, n_used,
                     w_exp_gate[i].reshape(N_EXPERTS, d, f).astype(BF16),
                     w_exp_up[i].reshape(N_EXPERTS, d, f).astype(BF16),
                     w_exp_down[i].reshape(N_EXPERTS, f, d).astype(BF16))
        yy = jnp.take(y, pos, axis=0).reshape(t, TOP_K * d)
        x = _ple(x, yy, p[i].reshape(t, PLE_DIM), ple_norm[i],
                 w_ple_gate_down[i].astype(BF16), w_ple_gate_up[i].astype(BF16), w_ple_proj[i].astype(BF16))

        if i == N_A_LAYERS - 1:
            gains = jnp.concatenate([_head_gain(k_norm_b, N_KV_HEADS), ones_kv])
            kv3 = _norm_proj(x, kv_norm_b, w_kv_b.astype(BF16), gains, 1)
            k2_b = _band_kv(kv3[0], B_PREV_CHUNKS)
            v2_b = _band_kv(kv3[1], B_PREV_CHUNKS)
    return x.reshape(b, s, d)
```

```python
import functools

import numpy as np
import jax
import jax.numpy as jnp
from jax import lax
from jax.experimental import pallas as pl
from jax.experimental.pallas import tpu as pltpu

D_MODEL = 4096
BATCH = 8
SEQ = 2048
DEPTH = 4
CHUNK = 64
N_A_LAYERS = DEPTH // 2
HEAD_DIM = 64
N_HEADS = D_MODEL // HEAD_DIM
N_KV_HEADS = 8
KV_GROUP = N_HEADS // N_KV_HEADS
KV_WIDTH = N_KV_HEADS * HEAD_DIM
WIN_CHUNKS = 2
B_PREV_CHUNKS = 8
N_BUCKETS = 32
T5_MAX_DIST = 128
REL_CLIP = 128
N_GROUPS = 4
N_EXPERTS_PER_GROUP = 8
N_EXPERTS = N_GROUPS * N_EXPERTS_PER_GROUP
EXPERT_FF = 384
TOP_K = 2
PLE_DIM = 256
RMS_EPS = 1e-6
NEG_INF = -1e30

LANES = 128
HEAD_TILE = KV_GROUP * HEAD_DIM
ROUTER_LANES = LANES
VMEM_LIMIT = 56 * 1024 * 1024

TM_PROJ = 512
TM_PLE = 256
TM_EXPERT = 256

F32 = jnp.float32
BF16 = jnp.bfloat16
CONST_BLOCK = pl.Buffered(1)


def _params(*sem):
    return pltpu.CompilerParams(dimension_semantics=sem, vmem_limit_bytes=VMEM_LIMIT)


def _rms(x, g):
    return x * lax.rsqrt(jnp.mean(x * x, axis=-1, keepdims=True) + RMS_EPS) * g


def _rows(table, idx):
    return table.at[idx].get(mode="promise_in_bounds")


def _norm_proj_kernel(x_ref, g_ref, w_ref, cg_ref, bd_ref, o_ref, h_ref, *, n_norm_tiles):
    j = pl.program_id(1)

    @pl.when(j == 0)
    def _():
        h_ref[...] = _rms(x_ref[...], g_ref[...]).astype(BF16)

    acc = jnp.dot(h_ref[...], w_ref[...], preferred_element_type=F32)

    @pl.when(j < n_norm_tiles)
    def _():
        ss = jnp.dot((acc * acc).astype(BF16), bd_ref[...], preferred_element_type=F32)
        o_ref[0] = (acc * lax.rsqrt(ss * (1.0 / HEAD_DIM) + RMS_EPS) * cg_ref[...]).astype(o_ref.dtype)

    @pl.when(j >= n_norm_tiles)
    def _():
        o_ref[0] = acc.astype(o_ref.dtype)


def _norm_proj(x, g, w, col_gain, n_norm_tiles):
    t, d = x.shape
    n = w.shape[1]
    tn = HEAD_TILE
    bd = jnp.asarray(np.kron(np.eye(tn // HEAD_DIM), np.ones((HEAD_DIM, HEAD_DIM))), BF16)
    return pl.pallas_call(
        functools.partial(_norm_proj_kernel, n_norm_tiles=n_norm_tiles),
        grid=(t // TM_PROJ, n // tn),
        in_specs=[
            pl.BlockSpec((TM_PROJ, d), lambda i, j: (i, 0)),
            pl.BlockSpec((1, d), lambda i, j: (0, 0), pipeline_mode=CONST_BLOCK),
            pl.BlockSpec((d, tn), lambda i, j: (0, j)),
            pl.BlockSpec((1, tn), lambda i, j: (0, j)),
            pl.BlockSpec((tn, tn), lambda i, j: (0, 0), pipeline_mode=CONST_BLOCK),
        ],
        out_specs=pl.BlockSpec((1, TM_PROJ, tn), lambda i, j: (j, i, 0)),
        out_shape=jax.ShapeDtypeStruct((n // tn, t, tn), BF16),
        scratch_shapes=[pltpu.VMEM((TM_PROJ, d), BF16)],
        compiler_params=_params("parallel", "arbitrary"),
        name="norm_proj",
    )(x, g.reshape(1, d), w, col_gain.reshape(1, n), bd)


def _attn_kernel(q_ref, k_ref, v_ref, bias_ref, o_ref, *, n_prev, has_sink):
    c = pl.program_id(1)
    band = (n_prev + 1) * CHUNK
    width = bias_ref.shape[-1]
    row0 = pl.multiple_of(c * CHUNK, CHUNK)
    kslot = lax.broadcasted_iota(jnp.int32, (1, width), 1)
    first_valid = (n_prev - c) * CHUNK
    keep = (kslot >= first_valid) & (kslot < band)
    lo = lax.broadcasted_iota(jnp.int32, (CHUNK, LANES), 1) < HEAD_DIM
    pairs = KV_GROUP // 2
    for h in range(N_KV_HEADS):
        qh = q_ref[h]
        zero = jnp.zeros((CHUNK, LANES), qh.dtype)
        parts = []
        for gg in range(pairs):
            qp = qh[:, gg * LANES:(gg + 1) * LANES]
            parts.append(jnp.where(lo, qp, zero))
            parts.append(jnp.where(lo, zero, qp))
        lhs = jnp.concatenate(parts, axis=0)
        k2 = k_ref[0, h, pl.ds(row0, width), :]
        v2 = v_ref[0, h, pl.ds(row0, width), :]
        s = lax.dot_general(lhs, k2, (((1,), (1,)), ((), ())), preferred_element_type=F32)
        bias = bias_ref[h * KV_GROUP:(h + 1) * KV_GROUP].reshape(KV_GROUP * CHUNK, width)
        if has_sink:
            s = jnp.where(keep, s, 0.0) + jnp.where(kslot < first_valid, NEG_INF, bias)
        else:
            s = jnp.where(keep, s + bias, NEG_INF)
        m = jnp.max(s, axis=-1, keepdims=True)
        p = jnp.exp(s - m)
        l = jnp.sum(p, axis=-1, keepdims=True)
        if has_sink:
            tail = jnp.where(kslot[:, width - LANES:] < band, p[:, width - LANES:], 0.0)
            p = jnp.concatenate([p[:, :width - LANES], tail], axis=1)
        o2 = jnp.dot(p.astype(BF16), v2, preferred_element_type=F32) * (1.0 / l)
        outs = []
        for gg in range(pairs):
            a = o2[(2 * gg) * CHUNK:(2 * gg + 1) * CHUNK]
            b = o2[(2 * gg + 1) * CHUNK:(2 * gg + 2) * CHUNK]
            outs.append(jnp.where(lo, a, b))
        o_ref[h] = jnp.concatenate(outs, axis=1).astype(o_ref.dtype)


def _attention(q3, k2, v2, bias, sinks, n_prev):
    t = q3.shape[1]
    nc = SEQ // CHUNK
    spad = k2.shape[2]
    has_sink = sinks is not None
    fill = jnp.full((N_HEADS, CHUNK, CHUNK), NEG_INF, F32)
    if has_sink:
        col = lax.broadcasted_iota(jnp.int32, fill.shape, 2)
        fill = jnp.where(col == 0, sinks.astype(F32)[:, None, None], fill)
    bias_ext = jnp.concatenate([bias, fill], axis=-1)
    width = bias_ext.shape[-1]
    return pl.pallas_call(
        functools.partial(_attn_kernel, n_prev=n_prev, has_sink=has_sink),
        grid=(BATCH, nc),
        in_specs=[
            pl.BlockSpec((N_KV_HEADS, CHUNK, HEAD_TILE), lambda b, c: (0, b * nc + c, 0)),
            pl.BlockSpec((1, N_KV_HEADS, spad, LANES), lambda b, c: (b, 0, 0, 0)),
            pl.BlockSpec((1, N_KV_HEADS, spad, LANES), lambda b, c: (b, 0, 0, 0)),
            pl.BlockSpec((N_HEADS, CHUNK, width), lambda b, c: (0, 0, 0), pipeline_mode=CONST_BLOCK),
        ],
        out_specs=pl.BlockSpec((N_KV_HEADS, CHUNK, HEAD_TILE), lambda b, c: (0, b * nc + c, 0)),
        out_shape=jax.ShapeDtypeStruct((N_KV_HEADS, t, HEAD_TILE), BF16),
        compiler_params=_params("parallel", "arbitrary"),
        name="band_attention",
    )(q3, k2, v2, bias_ext)


def _band_kv(kv_tile, n_prev):
    x = kv_tile.reshape(BATCH, SEQ, N_KV_HEADS, HEAD_DIM).transpose(0, 2, 1, 3)
    x = jnp.concatenate([x, x], axis=-1)
    return jnp.pad(x, ((0, 0), (0, 0), (n_prev * CHUNK, CHUNK), (0, 0)))


def _out_proj_kernel(a_ref, w_ref, r_ref, o_ref):
    acc = r_ref[...]
    for h in range(N_KV_HEADS):
        acc = acc + jnp.dot(a_ref[h], w_ref[h], preferred_element_type=F32)
    o_ref[...] = acc


def _out_proj(a3, w, res):
    t, d = res.shape
    tn = HEAD_TILE
    w3 = w.reshape(N_KV_HEADS, HEAD_TILE, d)
    return pl.pallas_call(
        _out_proj_kernel,
        grid=(t // TM_PROJ, d // tn),
        in_specs=[
            pl.BlockSpec((N_KV_HEADS, TM_PROJ, HEAD_TILE), lambda i, j: (0, i, 0)),
            pl.BlockSpec((N_KV_HEADS, HEAD_TILE, tn), lambda i, j: (0, 0, j)),
            pl.BlockSpec((TM_PROJ, tn), lambda i, j: (i, j)),
        ],
        out_specs=pl.BlockSpec((TM_PROJ, tn), lambda i, j: (i, j)),
        out_shape=jax.ShapeDtypeStruct((t, d), F32),
        compiler_params=_params("parallel", "arbitrary"),
        name="out_proj",
    )(a3, w3, res)


def _router_kernel(x_ref, g_ref, whi_ref, wlo_ref, h_ref, route_ref):
    hf = _rms(x_ref[...], g_ref[...])
    hb = hf.astype(BF16)
    h_ref[...] = hb
    hlo = (hf - hb.astype(F32)).astype(BF16)
    whi = whi_ref[...]
    logits = (jnp.dot(hb, whi, preferred_element_type=F32)
              + jnp.dot(hb, wlo_ref[...], preferred_element_type=F32)
              + jnp.dot(hlo, whi, preferred_element_type=F32))
    lane = lax.broadcasted_iota(jnp.int32, logits.shape, 1).astype(F32)
    big = float(ROUTER_LANES)
    is_g = lane < N_GROUPS
    gl = jnp.where(is_g, logits, NEG_INF)
    gmax = jnp.max(gl, axis=-1, keepdims=True)
    gidx = jnp.min(jnp.where(gl == gmax, lane, big), axis=-1, keepdims=True)
    gsum = jnp.sum(jnp.where(is_g, jnp.exp(gl - gmax), 0.0), axis=-1, keepdims=True)
    g_w = 1.0 / gsum
    e0 = N_GROUPS + gidx * N_EXPERTS_PER_GROUP
    el = jnp.where((lane >= e0) & (lane < e0 + N_EXPERTS_PER_GROUP), logits, NEG_INF)
    m1 = jnp.max(el, axis=-1, keepdims=True)
    i1 = jnp.min(jnp.where(el == m1, lane, big), axis=-1, keepdims=True)
    el2 = jnp.where(lane == i1, NEG_INF, el)
    m2 = jnp.max(el2, axis=-1, keepdims=True)
    i2 = jnp.min(jnp.where(el2 == m2, lane, big), axis=-1, keepdims=True)
    ex = jnp.exp(m2 - m1)
    w1 = 1.0 / (1.0 + ex)
    w2 = ex * w1
    route = jnp.where(lane == 0, i1 - N_GROUPS,
                      jnp.where(lane == 1, i2 - N_GROUPS,
                                jnp.where(lane == 2, g_w * w1,
                                          jnp.where(lane == 3, g_w * w2, 0.0))))
    route_ref[...] = route


def _router(x, g, w_rg, w_re):
    t, d = x.shape
    w = jnp.concatenate([w_rg, w_re.transpose(1, 0, 2).reshape(d, N_EXPERTS)], axis=1)
    w = jnp.pad(w, ((0, 0), (0, ROUTER_LANES - w.shape[1])))
    whi = w.astype(BF16)
    wlo = (w - whi.astype(F32)).astype(BF16)
    return pl.pallas_call(
        _router_kernel,
        grid=(t // TM_PROJ,),
        in_specs=[
            pl.BlockSpec((TM_PROJ, d), lambda i: (i, 0)),
            pl.BlockSpec((1, d), lambda i: (0, 0), pipeline_mode=CONST_BLOCK),
            pl.BlockSpec((d, ROUTER_LANES), lambda i: (0, 0), pipeline_mode=CONST_BLOCK),
            pl.BlockSpec((d, ROUTER_LANES), lambda i: (0, 0), pipeline_mode=CONST_BLOCK),
        ],
        out_specs=[
            pl.BlockSpec((TM_PROJ, d), lambda i: (i, 0)),
            pl.BlockSpec((TM_PROJ, ROUTER_LANES), lambda i: (i, 0)),
        ],
        out_shape=[jax.ShapeDtypeStruct((t, d), BF16), jax.ShapeDtypeStruct((t, ROUTER_LANES), F32)],
        compiler_params=_params("parallel"),
        name="router",
    )(x, g.reshape(1, d), whi, wlo)


def _experts_kernel(tile_ref, exp_ref, lo_ref, hi_ref, nu_ref, xs_ref, cw_ref, wg_ref, wu_ref, wd_ref, y_ref):
    i = pl.program_id(0)

    @pl.when(i < nu_ref[0])
    def _():
        base = tile_ref[i] * TM_EXPERT
        row = base + lax.broadcasted_iota(jnp.int32, (TM_EXPERT, 1), 0)
        cw = jnp.where((row >= lo_ref[i]) & (row < hi_ref[i]), cw_ref[...], 0.0)
        xs = xs_ref[...]
        a = jnp.dot(xs, wg_ref[0], preferred_element_type=F32)
        u = jnp.dot(xs, wu_ref[0], preferred_element_type=F32)
        act = a * jax.nn.sigmoid(a) * u * cw
        r = jnp.dot(act.astype(BF16), wd_ref[0], preferred_element_type=F32)
        first = lo_ref[i] == base

        @pl.when(first)
        def _():
            y_ref[...] = r.astype(y_ref.dtype)

        @pl.when(jnp.logical_not(first))
        def _():
            y_ref[...] = (y_ref[...].astype(F32) + r).astype(y_ref.dtype)


def _experts(xs, cw, plan, wg, wu, wd, layer):
    n_rows, d = xs.shape
    f = wg.shape[2]
    tile, expert, lo, hi, n_used = plan
    n_items = tile.shape[0]
    w_idx = lambda i, tl, ex, lo_, hi_, nu: (ex[i] + layer * N_EXPERTS, 0, 0)
    row_idx = lambda i, tl, ex, lo_, hi_, nu: (tl[i], 0)
    grid_spec = pltpu.PrefetchScalarGridSpec(
        num_scalar_prefetch=5,
        grid=(n_items,),
        in_specs=[
            pl.BlockSpec((TM_EXPERT, d), row_idx),
            pl.BlockSpec((TM_EXPERT, 1), row_idx),
            pl.BlockSpec((1, d, f), w_idx),
            pl.BlockSpec((1, d, f), w_idx),
            pl.BlockSpec((1, f, d), w_idx),
        ],
        out_specs=pl.BlockSpec((TM_EXPERT, d), row_idx),
    )
    return pl.pallas_call(
        _experts_kernel,
        grid_spec=grid_spec,
        out_shape=jax.ShapeDtypeStruct((n_rows, d), BF16),
        compiler_params=_params("arbitrary"),
        name="experts",
    )(tile, expert, lo, hi, n_used, xs, cw, wg, wu, wd)


def _dispatch_plan(route):
    t = route.shape[0]
    n_assign = t * TOP_K
    n_tiles = n_assign // TM_EXPERT
    ef = route[:, :TOP_K].T.reshape(-1).astype(jnp.int32)
    wf = route[:, TOP_K:2 * TOP_K].T.reshape(-1)
    iota = jnp.arange(n_assign, dtype=jnp.int32)
    se, order, ws = lax.sort((ef, iota, wf), num_keys=1, is_stable=True)
    _, pos = lax.sort((order, iota), num_keys=1)
    src_token = order % t
    edges = jnp.arange(1, N_EXPERTS + 1, dtype=jnp.int32)
    ends = jnp.sum((se[None, :] < edges[:, None]).astype(jnp.int32), axis=1)
    cuts = jnp.sort(jnp.concatenate([jnp.arange(n_tiles, dtype=jnp.int32) * TM_EXPERT, ends[:-1]]))
    lo = cuts
    hi = jnp.concatenate([cuts[1:], jnp.full((1,), n_assign, jnp.int32)])
    nonempty = hi > lo
    lo, hi = lax.sort((jnp.where(nonempty, lo, n_assign), hi), num_keys=1)
    n_used = jnp.sum(nonempty.astype(jnp.int32)).reshape(1)
    lo_c = jnp.minimum(lo, n_assign - 1)
    tile = lo_c // TM_EXPERT
    expert = jnp.minimum(jnp.sum((ends[None, :] <= lo_c[:, None]).astype(jnp.int32), axis=1), N_EXPERTS - 1)
    return src_token, ws.reshape(n_assign, 1), (tile, expert, lo, hi, n_used), pos


def _ple_kernel(x_ref, yy_ref, p_ref, g_ref, wgd_ref, wgu_ref, wp_ref, o_ref):
    x = x_ref[...] + yy_ref[0].astype(F32) + yy_ref[1].astype(F32)
    hn = _rms(x, g_ref[...]).astype(BF16)
    tdown = jnp.dot(hn, wgd_ref[...], preferred_element_type=F32)
    gate = jax.nn.sigmoid(jnp.dot(tdown.astype(BF16), wgu_ref[...], preferred_element_type=F32))
    pp = jnp.dot(p_ref[...].astype(BF16), wp_ref[...], preferred_element_type=F32)
    o_ref[...] = x + gate * pp


def _ple(x, yy, p, g, wgd, wgu, wp):
    t, d = x.shape
    return pl.pallas_call(
        _ple_kernel,
        grid=(t // TM_PLE,),
        in_specs=[
            pl.BlockSpec((TM_PLE, d), lambda i: (i, 0)),
            pl.BlockSpec((TOP_K, TM_PLE, d), lambda i: (0, i, 0)),
            pl.BlockSpec((TM_PLE, PLE_DIM), lambda i: (i, 0)),
            pl.BlockSpec((1, d), lambda i: (0, 0), pipeline_mode=CONST_BLOCK),
            pl.BlockSpec((d, PLE_DIM), lambda i: (0, 0), pipeline_mode=CONST_BLOCK),
            pl.BlockSpec((PLE_DIM, d), lambda i: (0, 0), pipeline_mode=CONST_BLOCK),
            pl.BlockSpec((PLE_DIM, d), lambda i: (0, 0), pipeline_mode=CONST_BLOCK),
        ],
        out_specs=pl.BlockSpec((TM_PLE, d), lambda i: (i, 0)),
        out_shape=jax.ShapeDtypeStruct((t, d), F32),
        compiler_params=_params("parallel"),
        name="ple",
    )(x, yy, p, g.reshape(1, d), wgd, wgu, wp)


def _band_rel(n_prev):
    qi = np.arange(CHUNK)[:, None]
    kj = np.arange((n_prev + 1) * CHUNK)[None, :]
    return kj - n_prev * CHUNK - qi


def _t5_bucket(rel):
    nb = N_BUCKETS // 2
    n = -rel
    ret = np.where(n < 0, nb, 0)
    n = np.abs(n)
    max_exact = nb // 2
    large = max_exact + (np.log(np.maximum(n, 1) / max_exact)
                         / np.log(T5_MAX_DIST / max_exact) * (nb - max_exact)).astype(np.int32)
    large = np.minimum(large, nb - 1)
    return (ret + np.where(n < max_exact, n, large)).astype(np.int32)


def _head_gain(g, n_heads, scale=1.0):
    return jnp.tile(g.astype(F32) * scale, n_heads)


def kernel(x, p, t5_bias, attn_norm_a, w_qkv_a, q_norm_a, k_norm_a, sinks_a, w_o_a, kv_norm_b, w_kv_b, k_norm_b, attn_norm_b, w_q_b, q_norm_b, rel_bias_b, w_o_b, ffn_norm, w_router_group, w_router_expert, w_exp_gate, w_exp_up, w_exp_down, w_ple_proj, ple_norm, w_ple_gate_down, w_ple_gate_up):
    b, s, d = x.shape
    t = b * s
    f = w_exp_gate.shape[-1]
    x = x.reshape(t, d)
    scale = HEAD_DIM ** -0.5
    bias_a = t5_bias.astype(F32)[:, _t5_bucket(_band_rel(WIN_CHUNKS))]
    rel_idx = np.clip(-_band_rel(B_PREV_CHUNKS), -REL_CLIP, REL_CLIP) + REL_CLIP
    ones_kv = jnp.ones((KV_WIDTH,), F32)
    wg_all = w_exp_gate.astype(BF16).reshape(DEPTH * N_EXPERTS, d, f)
    wu_all = w_exp_up.astype(BF16).reshape(DEPTH * N_EXPERTS, d, f)
    wd_all = w_exp_down.astype(BF16).reshape(DEPTH * N_EXPERTS, f, d)
    k2_b = v2_b = None
    for i in range(DEPTH):
        if i < N_A_LAYERS:
            gains = jnp.concatenate([_head_gain(q_norm_a[i], N_HEADS, scale),
                                     _head_gain(k_norm_a[i], N_KV_HEADS), ones_kv])
            qkv3 = _norm_proj(x, attn_norm_a[i], w_qkv_a[i].astype(BF16), gains, N_KV_HEADS + 1)
            k2 = _band_kv(qkv3[N_KV_HEADS], WIN_CHUNKS)
            v2 = _band_kv(qkv3[N_KV_HEADS + 1], WIN_CHUNKS)
            o3 = _attention(qkv3, k2, v2, bias_a, sinks_a[i], WIN_CHUNKS)
            x = _out_proj(o3, w_o_a[i].astype(BF16), x)
        else:
            j = i - N_A_LAYERS
            q3 = _norm_proj(x, attn_norm_b[j], w_q_b[j].astype(BF16),
                            _head_gain(q_norm_b[j], N_HEADS, scale), N_KV_HEADS)
            bias_b = rel_bias_b[j].astype(F32)[:, rel_idx]
            o3 = _attention(q3, k2_b, v2_b, bias_b, None, B_PREV_CHUNKS)
            x = _out_proj(o3, w_o_b[j].astype(BF16), x)

        h, route = _router(x, ffn_norm[i], w_router_group[i], w_router_expert[i])
        src_token, cw, plan, pos = _dispatch_plan(route)
        xs = _rows(h, src_token)
        y = _experts(xs, cw, plan, wg_all, wu_all, wd_all, i)
        yy = _rows(y, pos).reshape(TOP_K, t, d)
        x = _ple(x, yy, p[i].reshape(t, PLE_DIM), ple_norm[i],
                 w_ple_gate_down[i].astype(BF16), w_ple_gate_up[i].astype(BF16), w_ple_proj[i].astype(BF16))

        if i == N_A_LAYERS - 1:
            gains = jnp.concatenate([_head_gain(k_norm_b, N_KV_HEADS), ones_kv])
            kv3 = _norm_proj(x, kv_norm_b, w_kv_b.astype(BF16), gains, 1)
            k2_b = _band_kv(kv3[0], B_PREV_CHUNKS)
            v2_b = _band_kv(kv3[1], B_PREV_CHUNKS)
    return x.reshape(b, s, d)
```

```python
import functools
import math

import numpy as np
import jax
import jax.numpy as jnp
from jax import lax
from jax.experimental import pallas as pl
from jax.experimental.pallas import tpu as pltpu

D_MODEL = 4096
BATCH = 8
SEQ = 2048
DEPTH = 4
CHUNK = 64
N_A_LAYERS = DEPTH // 2
HEAD_DIM = 64
N_HEADS = D_MODEL // HEAD_DIM
N_KV_HEADS = 8
KV_GROUP = N_HEADS // N_KV_HEADS
KV_WIDTH = N_KV_HEADS * HEAD_DIM
WIN_CHUNKS = 2
B_PREV_CHUNKS = 8
N_BUCKETS = 32
T5_MAX_DIST = 128
REL_CLIP = 128
N_GROUPS = 4
N_EXPERTS_PER_GROUP = 8
N_EXPERTS = N_GROUPS * N_EXPERTS_PER_GROUP
EXPERT_FF = 384
TOP_K = 2
PLE_DIM = 256
RMS_EPS = 1e-6
NEG_INF = -1e30
LOG2E = math.log2(math.e)

LANES = 128
MXU_TILE = 256
HEAD_TILE = KV_GROUP * HEAD_DIM
ROUTER_LANES = LANES
VMEM_LIMIT = 56 * 1024 * 1024

N_STREAMS = 2
TM_PROJ = 1024
SUB_ROWS = 256
TM_ROW = 512
TM_PLE = 256
TM_EXPERT = 256

F32 = jnp.float32
BF16 = jnp.bfloat16
CONST_BLOCK = pl.Buffered(1)


def _params(*sem):
    return pltpu.CompilerParams(dimension_semantics=sem, vmem_limit_bytes=VMEM_LIMIT)


def _rms(x, g):
    return x * lax.rsqrt(jnp.mean(x * x, axis=-1, keepdims=True) + RMS_EPS) * g


def _rows(table, idx):
    return table.at[idx].get(mode="promise_in_bounds")


def _rmsnorm_kernel(x_ref, g_ref, h_ref):
    h_ref[...] = _rms(x_ref[...], g_ref[...]).astype(h_ref.dtype)


def _rmsnorm(x, g, row0, n_rows):
    d = x.shape[1]
    off = row0 // TM_ROW
    return pl.pallas_call(
        _rmsnorm_kernel,
        grid=(n_rows // TM_ROW,),
        in_specs=[pl.BlockSpec((TM_ROW, d), lambda i: (i + off, 0)),
                  pl.BlockSpec((1, d), lambda i: (0, 0), pipeline_mode=CONST_BLOCK)],
        out_specs=pl.BlockSpec((TM_ROW, d), lambda i: (i, 0)),
        out_shape=jax.ShapeDtypeStruct((n_rows, d), BF16),
        compiler_params=_params("parallel"),
        name="rmsnorm",
    )(x, g.reshape(1, d))


def _proj_kernel(h_ref, w_ref, cg_ref, bd_ref, o_ref, *, n_norm_tiles, n_col_tiles):
    j = pl.program_id(1)
    tm, tn = h_ref.shape[0], w_ref.shape[1]

    def tile(normed):
        for r in range(tm // SUB_ROWS):
            rows = pl.ds(r * SUB_ROWS, SUB_ROWS)
            acc = jnp.dot(h_ref[rows, :], w_ref[...], preferred_element_type=F32)
            if normed:
                sq = (acc * acc).astype(BF16)
                ms = jnp.concatenate(
                    [jnp.dot(sq[:, c * MXU_TILE:(c + 1) * MXU_TILE], bd_ref[...], preferred_element_type=F32)
                     for c in range(tn // MXU_TILE)], axis=1)
                acc = acc * lax.rsqrt(ms + RMS_EPS) * cg_ref[...]
            o_ref[0, rows, :] = acc.astype(o_ref.dtype)

    if n_norm_tiles > 0:
        pl.when(j < n_norm_tiles)(lambda: tile(True))
    if n_norm_tiles < n_col_tiles:
        pl.when(j >= n_norm_tiles)(lambda: tile(False))


def _proj(h, w, col_gain, n_norm_tiles):
    t, d = h.shape
    n = w.shape[1]
    tn = HEAD_TILE
    bd = jnp.asarray(np.kron(np.eye(MXU_TILE // HEAD_DIM), np.full((HEAD_DIM, HEAD_DIM), 1.0 / HEAD_DIM)), BF16)
    return pl.pallas_call(
        functools.partial(_proj_kernel, n_norm_tiles=n_norm_tiles, n_col_tiles=n // tn),
        grid=(t // TM_PROJ, n // tn),
        in_specs=[
            pl.BlockSpec((TM_PROJ, d), lambda i, j: (i, 0)),
            pl.BlockSpec((d, tn), lambda i, j: (0, j)),
            pl.BlockSpec((1, tn), lambda i, j: (0, j)),
            pl.BlockSpec((MXU_TILE, MXU_TILE), lambda i, j: (0, 0), pipeline_mode=CONST_BLOCK),
        ],
        out_specs=pl.BlockSpec((1, TM_PROJ, tn), lambda i, j: (j, i, 0)),
        out_shape=jax.ShapeDtypeStruct((n // tn, t, tn), BF16),
        compiler_params=_params("parallel", "arbitrary"),
        name="proj",
    )(h, w, col_gain.reshape(1, n), bd)


def _attn_kernel(q_ref, k_ref, v_ref, bias_ref, o_ref, *, n_prev, has_sink):
    c = pl.program_id(1)
    band = (n_prev + 1) * CHUNK
    width = bias_ref.shape[-1]
    row0 = pl.multiple_of(c * CHUNK, CHUNK)
    kslot = lax.broadcasted_iota(jnp.int32, (1, width), 1)
    first_valid = (n_prev - c) * CHUNK
    keep = (kslot >= first_valid) & (kslot < band)
    lo = lax.broadcasted_iota(jnp.int32, (CHUNK, LANES), 1) < HEAD_DIM
    pairs = KV_GROUP // 2
    for h in range(N_KV_HEADS):
        qh = q_ref[h]
        zero = jnp.zeros((CHUNK, LANES), qh.dtype)
        parts = []
        for gg in range(pairs):
            qp = qh[:, gg * LANES:(gg + 1) * LANES]
            parts.append(jnp.where(lo, qp, zero))
            parts.append(jnp.where(lo, zero, qp))
        lhs = jnp.concatenate(parts, axis=0)
        k2 = k_ref[0, h, pl.ds(row0, width), :]
        v2 = v_ref[0, h, pl.ds(row0, width), :]
        s = lax.dot_general(lhs, k2, (((1,), (1,)), ((), ())), preferred_element_type=F32)
        bias = bias_ref[h * KV_GROUP:(h + 1) * KV_GROUP].reshape(KV_GROUP * CHUNK, width)
        if has_sink:
            s = jnp.where(keep, s, 0.0) + jnp.where(kslot < first_valid, NEG_INF, bias)
        else:
            s = jnp.where(keep, s + bias, NEG_INF)
        m = jnp.max(s, axis=-1, keepdims=True)
        p = jnp.exp2(s - m)
        l = jnp.sum(p, axis=-1, keepdims=True)
        if has_sink:
            tail = jnp.where(kslot[:, width - LANES:] < band, p[:, width - LANES:], 0.0)
            p = jnp.concatenate([p[:, :width - LANES], tail], axis=1)
        o2 = jnp.dot(p.astype(BF16), v2, preferred_element_type=F32) * (1.0 / l)
        outs = []
        for gg in range(pairs):
            a = o2[(2 * gg) * CHUNK:(2 * gg + 1) * CHUNK]
            b = o2[(2 * gg + 1) * CHUNK:(2 * gg + 2) * CHUNK]
            outs.append(jnp.where(lo, a, b))
        o_ref[h] = jnp.concatenate(outs, axis=1).astype(o_ref.dtype)


def _attention(q3, k2, v2, bias_ext, n_prev, has_sink):
    t = q3.shape[1]
    nb = k2.shape[0]
    nc = SEQ // CHUNK
    spad = k2.shape[2]
    width = bias_ext.shape[-1]
    return pl.pallas_call(
        functools.partial(_attn_kernel, n_prev=n_prev, has_sink=has_sink),
        grid=(nb, nc),
        in_specs=[
            pl.BlockSpec((N_KV_HEADS, CHUNK, HEAD_TILE), lambda b, c: (0, b * nc + c, 0)),
            pl.BlockSpec((1, N_KV_HEADS, spad, LANES), lambda b, c: (b, 0, 0, 0)),
            pl.BlockSpec((1, N_KV_HEADS, spad, LANES), lambda b, c: (b, 0, 0, 0)),
            pl.BlockSpec((N_HEADS, CHUNK, width), lambda b, c: (0, 0, 0), pipeline_mode=CONST_BLOCK),
        ],
        out_specs=pl.BlockSpec((N_KV_HEADS, CHUNK, HEAD_TILE), lambda b, c: (0, b * nc + c, 0)),
        out_shape=jax.ShapeDtypeStruct((N_KV_HEADS, t, HEAD_TILE), BF16),
        compiler_params=_params("parallel", "arbitrary"),
        name="band_attention",
    )(q3, k2, v2, bias_ext)


def _bias_ext(bias, sinks):
    fill = jnp.full((N_HEADS, CHUNK, CHUNK), NEG_INF, F32)
    if sinks is not None:
        col = lax.broadcasted_iota(jnp.int32, fill.shape, 2)
        fill = jnp.where(col == 0, sinks.astype(F32)[:, None, None] * LOG2E, fill)
    return jnp.concatenate([bias.astype(F32) * LOG2E, fill], axis=-1)


def _band_kv(kv_tile, n_prev):
    nb = kv_tile.shape[0] // SEQ
    x = kv_tile.reshape(nb, SEQ, N_KV_HEADS, HEAD_DIM).transpose(0, 2, 1, 3)
    x = jnp.concatenate([x, x], axis=-1)
    return jnp.pad(x, ((0, 0), (0, 0), (n_prev * CHUNK, CHUNK), (0, 0)))


def _out_proj_kernel(a_ref, w_ref, r_ref, o_ref):
    acc = r_ref[...]
    for h in range(N_KV_HEADS):
        acc = acc + jnp.dot(a_ref[h], w_ref[h], preferred_element_type=F32)
    o_ref[...] = acc


def _out_proj(a3, w, res, row0):
    t = a3.shape[1]
    d = res.shape[1]
    tn = HEAD_TILE
    off = row0 // TM_PROJ
    w3 = w.reshape(N_KV_HEADS, HEAD_TILE, d)
    return pl.pallas_call(
        _out_proj_kernel,
        grid=(t // TM_PROJ, d // tn),
        in_specs=[
            pl.BlockSpec((N_KV_HEADS, TM_PROJ, HEAD_TILE), lambda i, j: (0, i, 0)),
            pl.BlockSpec((N_KV_HEADS, HEAD_TILE, tn), lambda i, j: (0, 0, j)),
            pl.BlockSpec((TM_PROJ, tn), lambda i, j: (i + off, j)),
        ],
        out_specs=pl.BlockSpec((TM_PROJ, tn), lambda i, j: (i, j)),
        out_shape=jax.ShapeDtypeStruct((t, d), F32),
        compiler_params=_params("parallel", "arbitrary"),
        name="out_proj",
    )(a3, w3, res)


def _router_kernel(x_ref, g_ref, whi_ref, wlo_ref, h_ref, route_ref):
    hf = _rms(x_ref[...], g_ref[...])
    hb = hf.astype(BF16)
    h_ref[...] = hb
    hlo = (hf - hb.astype(F32)).astype(BF16)
    whi = whi_ref[...]
    logits = (jnp.dot(hb, whi, preferred_element_type=F32)
              + jnp.dot(hb, wlo_ref[...], preferred_element_type=F32)
              + jnp.dot(hlo, whi, preferred_element_type=F32))
    lane = lax.broadcasted_iota(jnp.int32, logits.shape, 1).astype(F32)
    big = float(ROUTER_LANES)
    is_g = lane < N_GROUPS
    gl = jnp.where(is_g, logits, NEG_INF)
    gmax = jnp.max(gl, axis=-1, keepdims=True)
    gidx = jnp.min(jnp.where(gl == gmax, lane, big), axis=-1, keepdims=True)
    gsum = jnp.sum(jnp.where(is_g, jnp.exp(gl - gmax), 0.0), axis=-1, keepdims=True)
    g_w = 1.0 / gsum
    e0 = N_GROUPS + gidx * N_EXPERTS_PER_GROUP
    el = jnp.where((lane >= e0) & (lane < e0 + N_EXPERTS_PER_GROUP), logits, NEG_INF)
    m1 = jnp.max(el, axis=-1, keepdims=True)
    i1 = jnp.min(jnp.where(el == m1, lane, big), axis=-1, keepdims=True)
    el2 = jnp.where(lane == i1, NEG_INF, el)
    m2 = jnp.max(el2, axis=-1, keepdims=True)
    i2 = jnp.min(jnp.where(el2 == m2, lane, big), axis=-1, keepdims=True)
    ex = jnp.exp(m2 - m1)
    w1 = 1.0 / (1.0 + ex)
    w2 = ex * w1
    route = jnp.where(lane == 0, i1 - N_GROUPS,
                      jnp.where(lane == 1, i2 - N_GROUPS,
                                jnp.where(lane == 2, g_w * w1,
                                          jnp.where(lane == 3, g_w * w2, 0.0))))
    route_ref[...] = route


def _router(x, g, whi, wlo):
    t, d = x.shape
    return pl.pallas_call(
        _router_kernel,
        grid=(t // TM_ROW,),
        in_specs=[
            pl.BlockSpec((TM_ROW, d), lambda i: (i, 0)),
            pl.BlockSpec((1, d), lambda i: (0, 0), pipeline_mode=CONST_BLOCK),
            pl.BlockSpec((d, ROUTER_LANES), lambda i: (0, 0), pipeline_mode=CONST_BLOCK),
            pl.BlockSpec((d, ROUTER_LANES), lambda i: (0, 0), pipeline_mode=CONST_BLOCK),
        ],
        out_specs=[
            pl.BlockSpec((TM_ROW, d), lambda i: (i, 0)),
            pl.BlockSpec((TM_ROW, ROUTER_LANES), lambda i: (i, 0)),
        ],
        out_shape=[jax.ShapeDtypeStruct((t, d), BF16), jax.ShapeDtypeStruct((t, ROUTER_LANES), F32)],
        compiler_params=_params("parallel"),
        name="router",
    )(x, g.reshape(1, d), whi, wlo)


def _router_weights(w_rg, w_re):
    d = w_rg.shape[0]
    w = jnp.concatenate([w_rg, w_re.transpose(1, 0, 2).reshape(d, N_EXPERTS)], axis=1)
    w = jnp.pad(w, ((0, 0), (0, ROUTER_LANES - w.shape[1])))
    whi = w.astype(BF16)
    return whi, (w - whi.astype(F32)).astype(BF16)


def _experts_kernel(tile_ref, exp_ref, lo_ref, hi_ref, nu_ref, xs_ref, cw_ref, wg_ref, wu_ref, wd_ref, y_ref):
    i = pl.program_id(0)

    @pl.when(i < nu_ref[0])
    def _():
        base = tile_ref[i] * TM_EXPERT
        first = lo_ref[i] == base
        row = base + lax.broadcasted_iota(jnp.int32, (TM_EXPERT, 1), 0)
        cw = jnp.where((row >= lo_ref[i]) & (row < hi_ref[i]), cw_ref[...], 0.0)
        xs = xs_ref[...]
        a = jnp.dot(xs, wg_ref[0], preferred_element_type=F32)
        u = jnp.dot(xs, wu_ref[0], preferred_element_type=F32)
        act = a * jax.nn.sigmoid(a) * u * cw
        r = jnp.dot(act.astype(BF16), wd_ref[0], preferred_element_type=F32)

        @pl.when(first)
        def _():
            y_ref[...] = r.astype(y_ref.dtype)

        @pl.when(jnp.logical_not(first))
        def _():
            y_ref[...] = (y_ref[...].astype(F32) + r).astype(y_ref.dtype)


def _experts(xs, cw, plan, wg, wu, wd, layer):
    n_rows, d = xs.shape
    f = wg.shape[2]
    tile, expert, lo, hi, n_used = plan
    n_items = tile.shape[0]
    w_idx = lambda i, tl, ex, lo_, hi_, nu: (ex[i] + layer * N_EXPERTS, 0, 0)
    row_idx = lambda i, tl, ex, lo_, hi_, nu: (tl[i], 0)
    grid_spec = pltpu.PrefetchScalarGridSpec(
        num_scalar_prefetch=5,
        grid=(n_items,),
        in_specs=[
            pl.BlockSpec((TM_EXPERT, d), row_idx),
            pl.BlockSpec((TM_EXPERT, 1), row_idx),
            pl.BlockSpec((1, d, f), w_idx),
            pl.BlockSpec((1, d, f), w_idx),
            pl.BlockSpec((1, f, d), w_idx),
        ],
        out_specs=pl.BlockSpec((TM_EXPERT, d), row_idx),
    )
    return pl.pallas_call(
        _experts_kernel,
        grid_spec=grid_spec,
        out_shape=jax.ShapeDtypeStruct((n_rows, d), BF16),
        compiler_params=_params("arbitrary"),
        name="experts",
    )(tile, expert, lo, hi, n_used, xs, cw, wg, wu, wd)


def _dispatch_plan(route):
    t = route.shape[0]
    n_assign = t * TOP_K
    n_tiles = n_assign // TM_EXPERT
    ef = route[:, :TOP_K].T.reshape(-1).astype(jnp.int32)
    wf = route[:, TOP_K:2 * TOP_K].T.reshape(-1)
    iota = jnp.arange(n_assign, dtype=jnp.int32)
    se, order, ws = lax.sort((ef, iota, wf), num_keys=1, is_stable=True)
    _, pos = lax.sort((order, iota), num_keys=1)
    src_token = order % t
    edges = jnp.arange(1, N_EXPERTS + 1, dtype=jnp.int32)
    ends = jnp.sum((se[None, :] < edges[:, None]).astype(jnp.int32), axis=1)
    cuts = jnp.sort(jnp.concatenate([jnp.arange(n_tiles, dtype=jnp.int32) * TM_EXPERT, ends[:-1]]))
    lo = cuts
    hi = jnp.concatenate([cuts[1:], jnp.full((1,), n_assign, jnp.int32)])
    nonempty = hi > lo
    lo, hi = lax.sort((jnp.where(nonempty, lo, n_assign), hi), num_keys=1)
    n_used = jnp.sum(nonempty.astype(jnp.int32)).reshape(1)
    lo_c = jnp.minimum(lo, n_assign - 1)
    tile = lo_c // TM_EXPERT
    expert = jnp.minimum(jnp.sum((ends[None, :] <= lo_c[:, None]).astype(jnp.int32), axis=1), N_EXPERTS - 1)
    return src_token, ws.reshape(n_assign, 1), (tile, expert, lo, hi, n_used), pos


def _ple_kernel(x_ref, yy_ref, p_ref, g_ref, wgd_ref, wgu_ref, wp_ref, ng_ref, o_ref, *h_refs):
    x = x_ref[...] + yy_ref[0].astype(F32) + yy_ref[1].astype(F32)
    hn = _rms(x, g_ref[...]).astype(BF16)
    tdown = jnp.dot(hn, wgd_ref[...], preferred_element_type=F32)
    gate = jax.nn.sigmoid(jnp.dot(tdown.astype(BF16), wgu_ref[...], preferred_element_type=F32))
    pp = jnp.dot(p_ref[...].astype(BF16), wp_ref[...], preferred_element_type=F32)
    out = x + gate * pp
    o_ref[...] = out
    if h_refs:
        normed = out * lax.rsqrt(jnp.mean(out * out, axis=-1, keepdims=True) + RMS_EPS)
        for k, h_ref in enumerate(h_refs):
            h_ref[...] = (normed * ng_ref[k:k + 1, :]).astype(h_ref.dtype)


def _ple(x, yy, p, p_row0, g, wgd, wgu, wp, next_gains):
    t, d = x.shape
    off = p_row0 // TM_PLE
    n_next = len(next_gains)
    ng = jnp.stack(next_gains) if n_next else jnp.ones((1, d), F32)
    row_spec = pl.BlockSpec((TM_PLE, d), lambda i: (i, 0))
    return pl.pallas_call(
        _ple_kernel,
        grid=(t // TM_PLE,),
        in_specs=[
            row_spec,
            pl.BlockSpec((TOP_K, TM_PLE, d), lambda i: (0, i, 0)),
            pl.BlockSpec((TM_PLE, PLE_DIM), lambda i: (i + off, 0)),
            pl.BlockSpec((1, d), lambda i: (0, 0), pipeline_mode=CONST_BLOCK),
            pl.BlockSpec((d, PLE_DIM), lambda i: (0, 0), pipeline_mode=CONST_BLOCK),
            pl.BlockSpec((PLE_DIM, d), lambda i: (0, 0), pipeline_mode=CONST_BLOCK),
            pl.BlockSpec((PLE_DIM, d), lambda i: (0, 0), pipeline_mode=CONST_BLOCK),
            pl.BlockSpec(ng.shape, lambda i: (0, 0), pipeline_mode=CONST_BLOCK),
        ],
        out_specs=[row_spec] * (1 + n_next),
        out_shape=[jax.ShapeDtypeStruct((t, d), F32)] + [jax.ShapeDtypeStruct((t, d), BF16)] * n_next,
        compiler_params=_params("parallel"),
        name="ple",
    )(x, yy, p, g.reshape(1, d), wgd, wgu, wp, ng)


def _band_rel(n_prev):
    qi = np.arange(CHUNK)[:, None]
    kj = np.arange((n_prev + 1) * CHUNK)[None, :]
    return kj - n_prev * CHUNK - qi


def _t5_bucket(rel):
    nb = N_BUCKETS // 2
    n = -rel
    ret = np.where(n < 0, nb, 0)
    n = np.abs(n)
    max_exact = nb // 2
    large = max_exact + (np.log(np.maximum(n, 1) / max_exact)
                         / np.log(T5_MAX_DIST / max_exact) * (nb - max_exact)).astype(np.int32)
    large = np.minimum(large, nb - 1)
    return (ret + np.where(n < max_exact, n, large)).astype(np.int32)


def _head_gain(g, n_heads, scale=1.0):
    return jnp.tile(g.astype(F32) * scale, n_heads)


def kernel(x, p, t5_bias, attn_norm_a, w_qkv_a, q_norm_a, k_norm_a, sinks_a, w_o_a, kv_norm_b, w_kv_b, k_norm_b, attn_norm_b, w_q_b, q_norm_b, rel_bias_b, w_o_b, ffn_norm, w_router_group, w_router_expert, w_exp_gate, w_exp_up, w_exp_down, w_ple_proj, ple_norm, w_ple_gate_down, w_ple_gate_up):
    b, s, d = x.shape
    t = b * s
    ts = t // N_STREAMS
    f = w_exp_gate.shape[-1]
    x_in = x.reshape(t, d)
    p_flat = p.reshape(DEPTH * t, PLE_DIM)
    q_scale = HEAD_DIM ** -0.5 * LOG2E
    bias_a = t5_bias.astype(F32)[:, _t5_bucket(_band_rel(WIN_CHUNKS))]
    rel_idx = np.clip(-_band_rel(B_PREV_CHUNKS), -REL_CLIP, REL_CLIP) + REL_CLIP
    ones_kv = jnp.ones((KV_WIDTH,), F32)
    wg_all = w_exp_gate.astype(BF16).reshape(DEPTH * N_EXPERTS, d, f)
    wu_all = w_exp_up.astype(BF16).reshape(DEPTH * N_EXPERTS, d, f)
    wd_all = w_exp_down.astype(BF16).reshape(DEPTH * N_EXPERTS, f, d)
    streams = range(N_STREAMS)
    xs_ = [None] * N_STREAMS
    hs_ = [_rmsnorm(x_in, attn_norm_a[0], k * ts, ts) for k in streams]
    kv_b = [None] * N_STREAMS
    for i in range(DEPTH):
        is_a = i < N_A_LAYERS
        j = i - N_A_LAYERS
        if is_a:
            gains = jnp.concatenate([_head_gain(q_norm_a[i], N_HEADS, q_scale),
                                     _head_gain(k_norm_a[i], N_KV_HEADS), ones_kv])
            w_in, n_norm = w_qkv_a[i].astype(BF16), N_KV_HEADS + 1
            bias_ext = _bias_ext(bias_a, sinks_a[i])
            w_out = w_o_a[i].astype(BF16)
        else:
            gains = _head_gain(q_norm_b[j], N_HEADS, q_scale)
            w_in, n_norm = w_q_b[j].astype(BF16), N_KV_HEADS
            bias_ext = _bias_ext(rel_bias_b[j].astype(F32)[:, rel_idx], None)
            w_out = w_o_b[j].astype(BF16)
        whi, wlo = _router_weights(w_router_group[i], w_router_expert[i])
        wgd, wgu, wpp = (w_ple_gate_down[i].astype(BF16), w_ple_gate_up[i].astype(BF16),
                         w_ple_proj[i].astype(BF16))
        if i + 1 < DEPTH:
            next_gains = [attn_norm_a[i + 1] if i + 1 < N_A_LAYERS else attn_norm_b[i + 1 - N_A_LAYERS]]
            if i == N_A_LAYERS - 1:
                next_gains.append(kv_norm_b)
        else:
            next_gains = []

        q3 = [_proj(hs_[k], w_in, gains, n_norm) for k in streams]
        o3 = []
        for k in streams:
            if is_a:
                k2 = _band_kv(q3[k][N_KV_HEADS], WIN_CHUNKS)
                v2 = _band_kv(q3[k][N_KV_HEADS + 1], WIN_CHUNKS)
                o3.append(_attention(q3[k], k2, v2, bias_ext, WIN_CHUNKS, True))
            else:
                o3.append(_attention(q3[k], kv_b[k][0], kv_b[k][1], bias_ext, B_PREV_CHUNKS, False))
        for k in streams:
            if i == 0:
                xs_[k] = _out_proj(o3[k], w_out, x_in, k * ts)
            else:
                xs_[k] = _out_proj(o3[k], w_out, xs_[k], 0)
        routed = [_router(xs_[k], ffn_norm[i], whi, wlo) for k in streams]
        ys = []
        for k in streams:
            hk, route = routed[k]
            src_token, cw, plan, pos = _dispatch_plan(route)
            y = _experts(_rows(hk, src_token), cw, plan, wg_all, wu_all, wd_all, i)
            ys.append(_rows(y, pos).reshape(TOP_K, ts, d))
        for k in streams:
            outs = _ple(xs_[k], ys[k], p_flat, i * t + k * ts, ple_norm[i], wgd, wgu, wpp, next_gains)
            xs_[k] = outs[0]
            if next_gains:
                hs_[k] = outs[1]
            if i == N_A_LAYERS - 1:
                kv_gain = jnp.concatenate([_head_gain(k_norm_b, N_KV_HEADS), ones_kv])
                kv3 = _proj(outs[2], w_kv_b.astype(BF16), kv_gain, 1)
                kv_b[k] = (_band_kv(kv3[0], B_PREV_CHUNKS), _band_kv(kv3[1], B_PREV_CHUNKS))
    return jnp.concatenate(xs_, axis=0).reshape(b, s, d)
```

```python
import functools
import math

import numpy as np
import jax
import jax.numpy as jnp
from jax import lax
from jax.experimental import pallas as pl
from jax.experimental.pallas import tpu as pltpu

D_MODEL = 4096
BATCH = 8
SEQ = 2048
DEPTH = 4
CHUNK = 64
N_A_LAYERS = DEPTH // 2
HEAD_DIM = 64
N_HEADS = D_MODEL // HEAD_DIM
N_KV_HEADS = 8
KV_GROUP = N_HEADS // N_KV_HEADS
KV_WIDTH = N_KV_HEADS * HEAD_DIM
WIN_CHUNKS = 2
B_PREV_CHUNKS = 8
N_BUCKETS = 32
T5_MAX_DIST = 128
REL_CLIP = 128
N_GROUPS = 4
N_EXPERTS_PER_GROUP = 8
N_EXPERTS = N_GROUPS * N_EXPERTS_PER_GROUP
EXPERT_FF = 384
TOP_K = 2
PLE_DIM = 256
RMS_EPS = 1e-6
NEG_INF = -1e30
LOG2E = math.log2(math.e)

LANES = 128
MXU_TILE = 256
HEAD_TILE = KV_GROUP * HEAD_DIM
ROUTER_LANES = LANES
VMEM_LIMIT = 56 * 1024 * 1024

N_STREAMS = 2
TM_PROJ = 1024
SUB_ROWS = 256
TM_ROW = 512
TM_PLE = 256
TM_EXPERT = 256

F32 = jnp.float32
BF16 = jnp.bfloat16
CONST_BLOCK = pl.Buffered(1)


def _params(*sem):
    return pltpu.CompilerParams(dimension_semantics=sem, vmem_limit_bytes=VMEM_LIMIT)


def _rms(x, g):
    return x * lax.rsqrt(jnp.mean(x * x, axis=-1, keepdims=True) + RMS_EPS) * g


def _rows(table, idx):
    return table.at[idx].get(mode="promise_in_bounds")


def _rmsnorm_kernel(x_ref, g_ref, h_ref):
    h_ref[...] = _rms(x_ref[...], g_ref[...]).astype(h_ref.dtype)


def _rmsnorm(x, g, row0, n_rows):
    d = x.shape[1]
    off = row0 // TM_ROW
    return pl.pallas_call(
        _rmsnorm_kernel,
        grid=(n_rows // TM_ROW,),
        in_specs=[pl.BlockSpec((TM_ROW, d), lambda i: (i + off, 0)),
                  pl.BlockSpec((1, d), lambda i: (0, 0), pipeline_mode=CONST_BLOCK)],
        out_specs=pl.BlockSpec((TM_ROW, d), lambda i: (i, 0)),
        out_shape=jax.ShapeDtypeStruct((n_rows, d), BF16),
        compiler_params=_params("parallel"),
        name="rmsnorm",
    )(x, g.reshape(1, d))


def _proj_kernel(h_ref, w_ref, cg_ref, bd_ref, o_ref, *, n_norm_tiles, n_col_tiles):
    j = pl.program_id(1)
    tm, tn = h_ref.shape[0], w_ref.shape[1]

    def tile(normed):
        for r in range(tm // SUB_ROWS):
            rows = pl.ds(r * SUB_ROWS, SUB_ROWS)
            acc = jnp.dot(h_ref[rows, :], w_ref[...], preferred_element_type=F32)
            if normed:
                sq = (acc * acc).astype(BF16)
                ms = jnp.concatenate(
                    [jnp.dot(sq[:, c * MXU_TILE:(c + 1) * MXU_TILE], bd_ref[...], preferred_element_type=F32)
                     for c in range(tn // MXU_TILE)], axis=1)
                acc = acc * lax.rsqrt(ms + RMS_EPS) * cg_ref[...]
            o_ref[0, rows, :] = acc.astype(o_ref.dtype)

    if n_norm_tiles > 0:
        pl.when(j < n_norm_tiles)(lambda: tile(True))
    if n_norm_tiles < n_col_tiles:
        pl.when(j >= n_norm_tiles)(lambda: tile(False))


def _proj(h, w, col_gain, n_norm_tiles):
    t, d = h.shape
    n = w.shape[1]
    tn = HEAD_TILE
    bd = jnp.asarray(np.kron(np.eye(MXU_TILE // HEAD_DIM), np.full((HEAD_DIM, HEAD_DIM), 1.0 / HEAD_DIM)), BF16)
    return pl.pallas_call(
        functools.partial(_proj_kernel, n_norm_tiles=n_norm_tiles, n_col_tiles=n // tn),
        grid=(t // TM_PROJ, n // tn),
        in_specs=[
            pl.BlockSpec((TM_PROJ, d), lambda i, j: (i, 0)),
            pl.BlockSpec((d, tn), lambda i, j: (0, j)),
            pl.BlockSpec((1, tn), lambda i, j: (0, j)),
            pl.BlockSpec((MXU_TILE, MXU_TILE), lambda i, j: (0, 0), pipeline_mode=CONST_BLOCK),
        ],
        out_specs=pl.BlockSpec((1, TM_PROJ, tn), lambda i, j: (j, i, 0)),
        out_shape=jax.ShapeDtypeStruct((n // tn, t, tn), BF16),
        compiler_params=_params("parallel", "arbitrary"),
        name="proj",
    )(h, w, col_gain.reshape(1, n), bd)


def _attn_kernel(q_ref, kin_ref, vin_ref, bias_ref, o_ref, k_scr, v_scr, *, n_prev, has_sink):
    c = pl.program_id(1)
    band = (n_prev + 1) * CHUNK
    pad = n_prev * CHUNK
    width = bias_ref.shape[-1]
    lo = lax.broadcasted_iota(jnp.int32, (CHUNK, LANES), 1) < HEAD_DIM

    @pl.when(c == 0)
    def _():
        lo_s = lax.broadcasted_iota(jnp.int32, (SEQ, LANES), 1) < HEAD_DIM
        for src, dst in ((kin_ref, k_scr), (vin_ref, v_scr)):
            dst[:, :pad, :] = jnp.zeros((N_KV_HEADS, pad, LANES), dst.dtype)
            dst[:, pad + SEQ:, :] = jnp.zeros((N_KV_HEADS, CHUNK, LANES), dst.dtype)
            for hp in range(N_KV_HEADS // 2):
                two = src[0, :, hp * LANES:(hp + 1) * LANES].astype(F32)
                swapped = pltpu.roll(two, HEAD_DIM, axis=1)
                dst[2 * hp, pad:pad + SEQ, :] = jnp.where(lo_s, two, swapped).astype(dst.dtype)
                dst[2 * hp + 1, pad:pad + SEQ, :] = jnp.where(lo_s, swapped, two).astype(dst.dtype)

    row0 = pl.multiple_of(c * CHUNK, CHUNK)
    kslot = lax.broadcasted_iota(jnp.int32, (1, width), 1)
    first_valid = (n_prev - c) * CHUNK
    keep = (kslot >= first_valid) & (kslot < band)
    pairs = KV_GROUP // 2
    for h in range(N_KV_HEADS):
        qh = q_ref[h]
        zero = jnp.zeros((CHUNK, LANES), qh.dtype)
        parts = []
        for gg in range(pairs):
            qp = qh[:, gg * LANES:(gg + 1) * LANES]
            parts.append(jnp.where(lo, qp, zero))
            parts.append(jnp.where(lo, zero, qp))
        lhs = jnp.concatenate(parts, axis=0)
        k2 = k_scr[h, pl.ds(row0, width), :]
        v2 = v_scr[h, pl.ds(row0, width), :]
        s = lax.dot_general(lhs, k2, (((1,), (1,)), ((), ())), preferred_element_type=F32)
        bias = bias_ref[h * KV_GROUP:(h + 1) * KV_GROUP].reshape(KV_GROUP * CHUNK, width)
        if has_sink:
            s = jnp.where(keep, s, 0.0) + jnp.where(kslot < first_valid, NEG_INF, bias)
        else:
            s = jnp.where(keep, s + bias, NEG_INF)
        m = jnp.max(s, axis=-1, keepdims=True)
        p = jnp.exp2(s - m)
        l = jnp.sum(p, axis=-1, keepdims=True)
        if has_sink:
            tail = jnp.where(kslot[:, width - LANES:] < band, p[:, width - LANES:], 0.0)
            p = jnp.concatenate([p[:, :width - LANES], tail], axis=1)
        o2 = jnp.dot(p.astype(BF16), v2, preferred_element_type=F32) * (1.0 / l)
        outs = []
        for gg in range(pairs):
            a = o2[(2 * gg) * CHUNK:(2 * gg + 1) * CHUNK]
            b = o2[(2 * gg + 1) * CHUNK:(2 * gg + 2) * CHUNK]
            outs.append(jnp.where(lo, a, b))
        o_ref[h] = jnp.concatenate(outs, axis=1).astype(o_ref.dtype)


def _attention(q3, kv3, k_tile, v_tile, bias_ext, n_prev, has_sink):
    t = q3.shape[1]
    nb = t // SEQ
    nc = SEQ // CHUNK
    width = bias_ext.shape[-1]
    spad = n_prev * CHUNK + SEQ + CHUNK
    return pl.pallas_call(
        functools.partial(_attn_kernel, n_prev=n_prev, has_sink=has_sink),
        grid=(nb, nc),
        in_specs=[
            pl.BlockSpec((N_KV_HEADS, CHUNK, HEAD_TILE), lambda b, c: (0, b * nc + c, 0)),
            pl.BlockSpec((1, SEQ, KV_WIDTH), lambda b, c: (k_tile, b, 0)),
            pl.BlockSpec((1, SEQ, KV_WIDTH), lambda b, c: (v_tile, b, 0)),
            pl.BlockSpec((N_HEADS, CHUNK, width), lambda b, c: (0, 0, 0), pipeline_mode=CONST_BLOCK),
        ],
        out_specs=pl.BlockSpec((N_KV_HEADS, CHUNK, HEAD_TILE), lambda b, c: (0, b * nc + c, 0)),
        out_shape=jax.ShapeDtypeStruct((N_KV_HEADS, t, HEAD_TILE), BF16),
        scratch_shapes=[pltpu.VMEM((N_KV_HEADS, spad, LANES), BF16), pltpu.VMEM((N_KV_HEADS, spad, LANES), BF16)],
        compiler_params=_params("parallel", "arbitrary"),
        name="band_attention",
    )(q3, kv3, kv3, bias_ext)


def _bias_ext(bias, sinks):
    fill = jnp.full((N_HEADS, CHUNK, CHUNK), NEG_INF, F32)
    if sinks is not None:
        col = lax.broadcasted_iota(jnp.int32, fill.shape, 2)
        fill = jnp.where(col == 0, sinks.astype(F32)[:, None, None] * LOG2E, fill)
    return jnp.concatenate([bias.astype(F32) * LOG2E, fill], axis=-1)


def _out_proj_kernel(a_ref, w_ref, r_ref, o_ref):
    acc = r_ref[...]
    for h in range(N_KV_HEADS):
        acc = acc + jnp.dot(a_ref[h], w_ref[h], preferred_element_type=F32)
    o_ref[...] = acc


def _out_proj(a3, w, res, row0):
    t = a3.shape[1]
    d = res.shape[1]
    tn = HEAD_TILE
    off = row0 // TM_PROJ
    w3 = w.reshape(N_KV_HEADS, HEAD_TILE, d)
    return pl.pallas_call(
        _out_proj_kernel,
        grid=(t // TM_PROJ, d // tn),
        in_specs=[
            pl.BlockSpec((N_KV_HEADS, TM_PROJ, HEAD_TILE), lambda i, j: (0, i, 0)),
            pl.BlockSpec((N_KV_HEADS, HEAD_TILE, tn), lambda i, j: (0, 0, j)),
            pl.BlockSpec((TM_PROJ, tn), lambda i, j: (i + off, j)),
        ],
        out_specs=pl.BlockSpec((TM_PROJ, tn), lambda i, j: (i, j)),
        out_shape=jax.ShapeDtypeStruct((t, d), F32),
        compiler_params=_params("parallel", "arbitrary"),
        name="out_proj",
    )(a3, w3, res)


def _router_kernel(x_ref, g_ref, whi_ref, wlo_ref, h_ref, route_ref):
    hf = _rms(x_ref[...], g_ref[...])
    hb = hf.astype(BF16)
    h_ref[...] = hb
    hlo = (hf - hb.astype(F32)).astype(BF16)
    whi = whi_ref[...]
    logits = (jnp.dot(hb, whi, preferred_element_type=F32)
              + jnp.dot(hb, wlo_ref[...], preferred_element_type=F32)
              + jnp.dot(hlo, whi, preferred_element_type=F32))
    lane = lax.broadcasted_iota(jnp.int32, logits.shape, 1).astype(F32)
    big = float(ROUTER_LANES)
    is_g = lane < N_GROUPS
    gl = jnp.where(is_g, logits, NEG_INF)
    gmax = jnp.max(gl, axis=-1, keepdims=True)
    gidx = jnp.min(jnp.where(gl == gmax, lane, big), axis=-1, keepdims=True)
    gsum = jnp.sum(jnp.where(is_g, jnp.exp(gl - gmax), 0.0), axis=-1, keepdims=True)
    g_w = 1.0 / gsum
    e0 = N_GROUPS + gidx * N_EXPERTS_PER_GROUP
    el = jnp.where((lane >= e0) & (lane < e0 + N_EXPERTS_PER_GROUP), logits, NEG_INF)
    m1 = jnp.max(el, axis=-1, keepdims=True)
    i1 = jnp.min(jnp.where(el == m1, lane, big), axis=-1, keepdims=True)
    el2 = jnp.where(lane == i1, NEG_INF, el)
    m2 = jnp.max(el2, axis=-1, keepdims=True)
    i2 = jnp.min(jnp.where(el2 == m2, lane, big), axis=-1, keepdims=True)
    ex = jnp.exp(m2 - m1)
    w1 = 1.0 / (1.0 + ex)
    w2 = ex * w1
    route = jnp.where(lane == 0, i1 - N_GROUPS,
                      jnp.where(lane == 1, i2 - N_GROUPS,
                                jnp.where(lane == 2, g_w * w1,
                                          jnp.where(lane == 3, g_w * w2, 0.0))))
    route_ref[...] = route


def _router(x, g, whi, wlo):
    t, d = x.shape
    return pl.pallas_call(
        _router_kernel,
        grid=(t // TM_ROW,),
        in_specs=[
            pl.BlockSpec((TM_ROW, d), lambda i: (i, 0)),
            pl.BlockSpec((1, d), lambda i: (0, 0), pipeline_mode=CONST_BLOCK),
            pl.BlockSpec((d, ROUTER_LANES), lambda i: (0, 0), pipeline_mode=CONST_BLOCK),
            pl.BlockSpec((d, ROUTER_LANES), lambda i: (0, 0), pipeline_mode=CONST_BLOCK),
        ],
        out_specs=[
            pl.BlockSpec((TM_ROW, d), lambda i: (i, 0)),
            pl.BlockSpec((TM_ROW, ROUTER_LANES), lambda i: (i, 0)),
        ],
        out_shape=[jax.ShapeDtypeStruct((t, d), BF16), jax.ShapeDtypeStruct((t, ROUTER_LANES), F32)],
        compiler_params=_params("parallel"),
        name="router",
    )(x, g.reshape(1, d), whi, wlo)


def _router_weights(w_rg, w_re):
    d = w_rg.shape[0]
    w = jnp.concatenate([w_rg, w_re.transpose(1, 0, 2).reshape(d, N_EXPERTS)], axis=1)
    w = jnp.pad(w, ((0, 0), (0, ROUTER_LANES - w.shape[1])))
    whi = w.astype(BF16)
    return whi, (w - whi.astype(F32)).astype(BF16)


def _experts_kernel(te_ref, nu_ref, xs_ref, cw_ref, wg_ref, wu_ref, wd_ref, y_ref):
    i = pl.program_id(0)

    @pl.when(i < nu_ref[0])
    def _():
        xs = xs_ref[...]
        a = jnp.dot(xs, wg_ref[0], preferred_element_type=F32)
        u = jnp.dot(xs, wu_ref[0], preferred_element_type=F32)
        act = a * jax.nn.sigmoid(a) * u * cw_ref[...]
        y_ref[...] = jnp.dot(act.astype(BF16), wd_ref[0], preferred_element_type=F32).astype(y_ref.dtype)

    @pl.when(i >= nu_ref[0])
    def _():
        y_ref[...] = jnp.zeros_like(y_ref)


def _experts(xs, cw, tile_expert, n_used, wg, wu, wd, layer):
    n_rows, d = xs.shape
    f = wg.shape[2]
    w_idx = lambda i, te, nu: (te[i] + layer * N_EXPERTS, 0, 0)
    used_idx = lambda i, te, nu: (jnp.minimum(i, nu[0] - 1), 0)
    grid_spec = pltpu.PrefetchScalarGridSpec(
        num_scalar_prefetch=2,
        grid=(n_rows // TM_EXPERT,),
        in_specs=[
            pl.BlockSpec((TM_EXPERT, d), used_idx),
            pl.BlockSpec((TM_EXPERT, 1), used_idx),
            pl.BlockSpec((1, d, f), w_idx),
            pl.BlockSpec((1, d, f), w_idx),
            pl.BlockSpec((1, f, d), w_idx),
        ],
        out_specs=pl.BlockSpec((TM_EXPERT, d), lambda i, te, nu: (i, 0)),
    )
    return pl.pallas_call(
        _experts_kernel,
        grid_spec=grid_spec,
        out_shape=jax.ShapeDtypeStruct((n_rows, d), BF16),
        compiler_params=_params("arbitrary"),
        name="experts",
    )(tile_expert, n_used, xs, cw, wg, wu, wd)


def _dispatch_plan(route):
    t = route.shape[0]
    n_assign = t * TOP_K
    n_fill = N_EXPERTS * TM_EXPERT
    n_rows = n_assign + n_fill
    ef = route[:, :TOP_K].T.reshape(-1).astype(jnp.int32)
    wf = route[:, TOP_K:2 * TOP_K].T.reshape(-1)
    iota = jnp.arange(n_assign, dtype=jnp.int32)
    se, order, ws = lax.sort((ef, iota, wf), num_keys=1, is_stable=True)
    edges = jnp.arange(1, N_EXPERTS + 1, dtype=jnp.int32)
    ends = jnp.sum((se[None, :] < edges[:, None]).astype(jnp.int32), axis=1)
    starts = jnp.concatenate([jnp.zeros((1,), jnp.int32), ends[:-1]])
    counts = ends - starts
    padded = (counts + TM_EXPERT - 1) // TM_EXPERT * TM_EXPERT
    pend = jnp.cumsum(padded)
    pstart = pend - padded
    shift = pstart - starts
    dshift = shift - jnp.concatenate([jnp.zeros((1,), jnp.int32), shift[:-1]])
    dest = iota + jnp.sum(jnp.where(iota[None, :] >= starts[:, None], dshift[:, None], 0), axis=0)
    jj = jnp.arange(TM_EXPERT, dtype=jnp.int32)[None, :]
    fill_key = jnp.where(jj < (padded - counts)[:, None], (pstart + counts)[:, None] + jj, n_rows).reshape(-1)
    keys = jnp.concatenate([dest, fill_key])
    toks = jnp.concatenate([order % t, jnp.zeros((n_fill,), jnp.int32)])
    wts = jnp.concatenate([ws, jnp.zeros((n_fill,), F32)])
    _, src_token, cw = lax.sort((keys, toks, wts), num_keys=1)
    _, pos = lax.sort((order, dest), num_keys=1)
    n_used = (pend[-1] // TM_EXPERT).reshape(1)
    tile_ids = jnp.arange(n_rows // TM_EXPERT, dtype=jnp.int32)
    te = jnp.sum((pend[None, :] // TM_EXPERT <= jnp.minimum(tile_ids, n_used - 1)[:, None]).astype(jnp.int32), axis=1)
    te = jnp.minimum(te, N_EXPERTS - 1)
    return src_token, cw.reshape(n_rows, 1), te, n_used, pos


def _ple_kernel(x_ref, yy_ref, p_ref, g_ref, wgd_ref, wgu_ref, wp_ref, ng_ref, o_ref, *h_refs):
    x = x_ref[...] + yy_ref[0].astype(F32) + yy_ref[1].astype(F32)
    hn = _rms(x, g_ref[...]).astype(BF16)
    tdown = jnp.dot(hn, wgd_ref[...], preferred_element_type=F32)
    gate = jax.nn.sigmoid(jnp.dot(tdown.astype(BF16), wgu_ref[...], preferred_element_type=F32))
    pp = jnp.dot(p_ref[...].astype(BF16), wp_ref[...], preferred_element_type=F32)
    out = x + gate * pp
    o_ref[...] = out
    if h_refs:
        normed = out * lax.rsqrt(jnp.mean(out * out, axis=-1, keepdims=True) + RMS_EPS)
        for k, h_ref in enumerate(h_refs):
            h_ref[...] = (normed * ng_ref[k:k + 1, :]).astype(h_ref.dtype)


def _ple(x, yy, p, p_row0, g, wgd, wgu, wp, next_gains):
    t, d = x.shape
    off = p_row0 // TM_PLE
    n_next = len(next_gains)
    ng = jnp.stack(next_gains) if n_next else jnp.ones((1, d), F32)
    row_spec = pl.BlockSpec((TM_PLE, d), lambda i: (i, 0))
    return pl.pallas_call(
        _ple_kernel,
        grid=(t // TM_PLE,),
        in_specs=[
            row_spec,
            pl.BlockSpec((TOP_K, TM_PLE, d), lambda i: (0, i, 0)),
            pl.BlockSpec((TM_PLE, PLE_DIM), lambda i: (i + off, 0)),
            pl.BlockSpec((1, d), lambda i: (0, 0), pipeline_mode=CONST_BLOCK),
            pl.BlockSpec((d, PLE_DIM), lambda i: (0, 0), pipeline_mode=CONST_BLOCK),
            pl.BlockSpec((PLE_DIM, d), lambda i: (0, 0), pipeline_mode=CONST_BLOCK),
            pl.BlockSpec((PLE_DIM, d), lambda i: (0, 0), pipeline_mode=CONST_BLOCK),
            pl.BlockSpec(ng.shape, lambda i: (0, 0), pipeline_mode=CONST_BLOCK),
        ],
        out_specs=[row_spec] * (1 + n_next),
        out_shape=[jax.ShapeDtypeStruct((t, d), F32)] + [jax.ShapeDtypeStruct((t, d), BF16)] * n_next,
        compiler_params=_params("parallel"),
        name="ple",
    )(x, yy, p, g.reshape(1, d), wgd, wgu, wp, ng)


def _band_rel(n_prev):
    qi = np.arange(CHUNK)[:, None]
    kj = np.arange((n_prev + 1) * CHUNK)[None, :]
    return kj - n_prev * CHUNK - qi


def _t5_bucket(rel):
    nb = N_BUCKETS // 2
    n = -rel
    ret = np.where(n < 0, nb, 0)
    n = np.abs(n)
    max_exact = nb // 2
    large = max_exact + (np.log(np.maximum(n, 1) / max_exact)
                         / np.log(T5_MAX_DIST / max_exact) * (nb - max_exact)).astype(np.int32)
    large = np.minimum(large, nb - 1)
    return (ret + np.where(n < max_exact, n, large)).astype(np.int32)


def _head_gain(g, n_heads, scale=1.0):
    return jnp.tile(g.astype(F32) * scale, n_heads)


def kernel(x, p, t5_bias, attn_norm_a, w_qkv_a, q_norm_a, k_norm_a, sinks_a, w_o_a, kv_norm_b, w_kv_b, k_norm_b, attn_norm_b, w_q_b, q_norm_b, rel_bias_b, w_o_b, ffn_norm, w_router_group, w_router_expert, w_exp_gate, w_exp_up, w_exp_down, w_ple_proj, ple_norm, w_ple_gate_down, w_ple_gate_up):
    b, s, d = x.shape
    t = b * s
    ts = t // N_STREAMS
    f = w_exp_gate.shape[-1]
    x_in = x.reshape(t, d)
    p_flat = p.reshape(DEPTH * t, PLE_DIM)
    q_scale = HEAD_DIM ** -0.5 * LOG2E
    bias_a = t5_bias.astype(F32)[:, _t5_bucket(_band_rel(WIN_CHUNKS))]
    rel_idx = np.clip(-_band_rel(B_PREV_CHUNKS), -REL_CLIP, REL_CLIP) + REL_CLIP
    ones_kv = jnp.ones((KV_WIDTH,), F32)
    wg_all = w_exp_gate.astype(BF16).reshape(DEPTH * N_EXPERTS, d, f)
    wu_all = w_exp_up.astype(BF16).reshape(DEPTH * N_EXPERTS, d, f)
    wd_all = w_exp_down.astype(BF16).reshape(DEPTH * N_EXPERTS, f, d)
    streams = range(N_STREAMS)
    xs_ = [None] * N_STREAMS
    hs_ = [_rmsnorm(x_in, attn_norm_a[0], k * ts, ts) for k in streams]
    kv_b = [None] * N_STREAMS
    for i in range(DEPTH):
        is_a = i < N_A_LAYERS
        j = i - N_A_LAYERS
        if is_a:
            gains = jnp.concatenate([_head_gain(q_norm_a[i], N_HEADS, q_scale),
                                     _head_gain(k_norm_a[i], N_KV_HEADS), ones_kv])
            w_in, n_norm = w_qkv_a[i].astype(BF16), N_KV_HEADS + 1
            bias_ext = _bias_ext(bias_a, sinks_a[i])
            w_out = w_o_a[i].astype(BF16)
        else:
            gains = _head_gain(q_norm_b[j], N_HEADS, q_scale)
            w_in, n_norm = w_q_b[j].astype(BF16), N_KV_HEADS
            bias_ext = _bias_ext(rel_bias_b[j].astype(F32)[:, rel_idx], None)
            w_out = w_o_b[j].astype(BF16)
        whi, wlo = _router_weights(w_router_group[i], w_router_expert[i])
        wgd, wgu, wpp = (w_ple_gate_down[i].astype(BF16), w_ple_gate_up[i].astype(BF16),
                         w_ple_proj[i].astype(BF16))
        if i + 1 < DEPTH:
            next_gains = [attn_norm_a[i + 1] if i + 1 < N_A_LAYERS else attn_norm_b[i + 1 - N_A_LAYERS]]
            if i == N_A_LAYERS - 1:
                next_gains.append(kv_norm_b)
        else:
            next_gains = []

        q3 = [_proj(hs_[k], w_in, gains, n_norm) for k in streams]
        o3 = []
        for k in streams:
            if is_a:
                o3.append(_attention(q3[k], q3[k], N_KV_HEADS, N_KV_HEADS + 1, bias_ext, WIN_CHUNKS, True))
            else:
                o3.append(_attention(q3[k], kv_b[k], 0, 1, bias_ext, B_PREV_CHUNKS, False))
        for k in streams:
            if i == 0:
                xs_[k] = _out_proj(o3[k], w_out, x_in, k * ts)
            else:
                xs_[k] = _out_proj(o3[k], w_out, xs_[k], 0)
        routed = [_router(xs_[k], ffn_norm[i], whi, wlo) for k in streams]
        ys = []
        for k in streams:
            hk, route = routed[k]
            src_token, cw, tile_expert, n_used, pos = _dispatch_plan(route)
            y = _experts(_rows(hk, src_token), cw, tile_expert, n_used, wg_all, wu_all, wd_all, i)
            ys.append(_rows(y, pos).reshape(TOP_K, ts, d))
        for k in streams:
            outs = _ple(xs_[k], ys[k], p_flat, i * t + k * ts, ple_norm[i], wgd, wgu, wpp, next_gains)
            xs_[k] = outs[0]
            if next_gains:
                hs_[k] = outs[1]
            if i == N_A_LAYERS - 1:
                kv_gain = jnp.concatenate([_head_gain(k_norm_b, N_KV_HEADS), ones_kv])
                kv_b[k] = _proj(outs[2], w_kv_b.astype(BF16), kv_gain, 1)
    return jnp.concatenate(xs_, axis=0).reshape(b, s, d)
```

```python
import functools
import math

import numpy as np
import jax
import jax.numpy as jnp
from jax import lax
from jax.experimental import pallas as pl
from jax.experimental.pallas import tpu as pltpu

D_MODEL = 4096
BATCH = 8
SEQ = 2048
DEPTH = 4
CHUNK = 64
N_A_LAYERS = DEPTH // 2
HEAD_DIM = 64
N_HEADS = D_MODEL // HEAD_DIM
N_KV_HEADS = 8
KV_GROUP = N_HEADS // N_KV_HEADS
KV_WIDTH = N_KV_HEADS * HEAD_DIM
WIN_CHUNKS = 2
B_PREV_CHUNKS = 8
N_BUCKETS = 32
T5_MAX_DIST = 128
REL_CLIP = 128
N_GROUPS = 4
N_EXPERTS_PER_GROUP = 8
N_EXPERTS = N_GROUPS * N_EXPERTS_PER_GROUP
EXPERT_FF = 384
TOP_K = 2
PLE_DIM = 256
RMS_EPS = 1e-6
NEG_INF = -1e30
LOG2E = math.log2(math.e)

LANES = 128
MXU_TILE = 256
HEAD_TILE = KV_GROUP * HEAD_DIM
ROUTER_LANES = LANES
VMEM_LIMIT = 56 * 1024 * 1024

N_STREAMS = 2
TM_PROJ = 1024
SUB_ROWS = 256
TM_ROW = 512
TM_PLE = 256
TM_EXPERT = 256

F32 = jnp.float32
BF16 = jnp.bfloat16
CONST_BLOCK = pl.Buffered(1)


def _params(*sem):
    return pltpu.CompilerParams(dimension_semantics=sem, vmem_limit_bytes=VMEM_LIMIT)


def _rms(x, g):
    return x * lax.rsqrt(jnp.mean(x * x, axis=-1, keepdims=True) + RMS_EPS) * g


def _rows(table, idx):
    return table.at[idx].get(mode="promise_in_bounds")


ORDER_SPEC = pl.BlockSpec(memory_space=pl.ANY)


def _ordered(kernel_fn, n_in):
    def ordered_kernel(*refs):
        return kernel_fn(*refs[:n_in], *refs[n_in + 1:])
    return ordered_kernel


def _rmsnorm_kernel(x_ref, g_ref, h_ref):
    h_ref[...] = _rms(x_ref[...], g_ref[...]).astype(h_ref.dtype)


def _rmsnorm(x, g, row0, n_rows):
    d = x.shape[1]
    off = row0 // TM_ROW
    return pl.pallas_call(
        _rmsnorm_kernel,
        grid=(n_rows // TM_ROW,),
        in_specs=[pl.BlockSpec((TM_ROW, d), lambda i: (i + off, 0)),
                  pl.BlockSpec((1, d), lambda i: (0, 0), pipeline_mode=CONST_BLOCK)],
        out_specs=pl.BlockSpec((TM_ROW, d), lambda i: (i, 0)),
        out_shape=jax.ShapeDtypeStruct((n_rows, d), BF16),
        compiler_params=_params("parallel"),
        name="rmsnorm",
    )(x, g.reshape(1, d))


def _proj_kernel(h_ref, w_ref, cg_ref, bd_ref, o_ref, *, n_norm_tiles, n_col_tiles):
    j = pl.program_id(1)
    tm, tn = h_ref.shape[0], w_ref.shape[1]

    def tile(normed):
        for r in range(tm // SUB_ROWS):
            rows = pl.ds(r * SUB_ROWS, SUB_ROWS)
            acc = jnp.dot(h_ref[rows, :], w_ref[...], preferred_element_type=F32)
            if normed:
                sq = (acc * acc).astype(BF16)
                ms = jnp.concatenate(
                    [jnp.dot(sq[:, c * MXU_TILE:(c + 1) * MXU_TILE], bd_ref[...], preferred_element_type=F32)
                     for c in range(tn // MXU_TILE)], axis=1)
                acc = acc * lax.rsqrt(ms + RMS_EPS) * cg_ref[...]
            o_ref[0, rows, :] = acc.astype(o_ref.dtype)

    if n_norm_tiles > 0:
        pl.when(j < n_norm_tiles)(lambda: tile(True))
    if n_norm_tiles < n_col_tiles:
        pl.when(j >= n_norm_tiles)(lambda: tile(False))


def _proj(h, w, col_gain, n_norm_tiles, after):
    t, d = h.shape
    n = w.shape[1]
    tn = HEAD_TILE
    bd = jnp.asarray(np.kron(np.eye(MXU_TILE // HEAD_DIM), np.full((HEAD_DIM, HEAD_DIM), 1.0 / HEAD_DIM)), BF16)
    return pl.pallas_call(
        _ordered(functools.partial(_proj_kernel, n_norm_tiles=n_norm_tiles, n_col_tiles=n // tn), 4),
        grid=(t // TM_PROJ, n // tn),
        in_specs=[
            pl.BlockSpec((TM_PROJ, d), lambda i, j: (i, 0)),
            pl.BlockSpec((d, tn), lambda i, j: (0, j)),
            pl.BlockSpec((1, tn), lambda i, j: (0, j)),
            pl.BlockSpec((MXU_TILE, MXU_TILE), lambda i, j: (0, 0), pipeline_mode=CONST_BLOCK),
            ORDER_SPEC,
        ],
        out_specs=pl.BlockSpec((1, TM_PROJ, tn), lambda i, j: (j, i, 0)),
        out_shape=jax.ShapeDtypeStruct((n // tn, t, tn), BF16),
        compiler_params=_params("parallel", "arbitrary"),
        name="proj",
    )(h, w, col_gain.reshape(1, n), bd, after)


def _attn_kernel(q_ref, kin_ref, vin_ref, bias_ref, o_ref, k_scr, v_scr, *, n_prev, has_sink):
    c = pl.program_id(1)
    band = (n_prev + 1) * CHUNK
    pad = n_prev * CHUNK
    width = bias_ref.shape[-1]
    lo = lax.broadcasted_iota(jnp.int32, (CHUNK, LANES), 1) < HEAD_DIM

    @pl.when(c == 0)
    def _():
        lo_s = lax.broadcasted_iota(jnp.int32, (SEQ, LANES), 1) < HEAD_DIM
        for src, dst in ((kin_ref, k_scr), (vin_ref, v_scr)):
            dst[:, :pad, :] = jnp.zeros((N_KV_HEADS, pad, LANES), dst.dtype)
            dst[:, pad + SEQ:, :] = jnp.zeros((N_KV_HEADS, CHUNK, LANES), dst.dtype)
            for hp in range(N_KV_HEADS // 2):
                two = src[0, :, hp * LANES:(hp + 1) * LANES].astype(F32)
                swapped = pltpu.roll(two, HEAD_DIM, axis=1)
                dst[2 * hp, pad:pad + SEQ, :] = jnp.where(lo_s, two, swapped).astype(dst.dtype)
                dst[2 * hp + 1, pad:pad + SEQ, :] = jnp.where(lo_s, swapped, two).astype(dst.dtype)

    row0 = pl.multiple_of(c * CHUNK, CHUNK)
    kslot = lax.broadcasted_iota(jnp.int32, (1, width), 1)
    first_valid = (n_prev - c) * CHUNK
    keep = (kslot >= first_valid) & (kslot < band)
    pairs = KV_GROUP // 2
    def one_kv_head(h):
        qh = q_ref[h]
        zero = jnp.zeros((CHUNK, LANES), qh.dtype)
        parts = []
        for gg in range(pairs):
            qp = qh[:, gg * LANES:(gg + 1) * LANES]
            parts.append(jnp.where(lo, qp, zero))
            parts.append(jnp.where(lo, zero, qp))
        lhs = jnp.concatenate(parts, axis=0)
        k2 = k_scr[h, pl.ds(row0, width), :]
        v2 = v_scr[h, pl.ds(row0, width), :]
        s = lax.dot_general(lhs, k2, (((1,), (1,)), ((), ())), preferred_element_type=F32)
        bias = bias_ref[h * KV_GROUP:(h + 1) * KV_GROUP].reshape(KV_GROUP * CHUNK, width)
        if has_sink:
            s = jnp.where(keep, s, 0.0) + jnp.where(kslot < first_valid, NEG_INF, bias)
        else:
            s = jnp.where(keep, s + bias, NEG_INF)
        m = jnp.max(s, axis=-1, keepdims=True)
        p = jnp.exp2(s - m)
        l = jnp.sum(p, axis=-1, keepdims=True)
        if has_sink:
            tail = jnp.where(kslot[:, width - LANES:] < band, p[:, width - LANES:], 0.0)
            p = jnp.concatenate([p[:, :width - LANES], tail], axis=1)
        o2 = jnp.dot(p.astype(BF16), v2, preferred_element_type=F32) * (1.0 / l)
        outs = []
        for gg in range(pairs):
            a = o2[(2 * gg) * CHUNK:(2 * gg + 1) * CHUNK]
            b = o2[(2 * gg + 1) * CHUNK:(2 * gg + 2) * CHUNK]
            outs.append(jnp.where(lo, a, b))
        o_ref[h] = jnp.concatenate(outs, axis=1).astype(o_ref.dtype)

    for h in range(N_KV_HEADS):
        one_kv_head(h)


def _attention(q3, kv3, k_tile, v_tile, bias_ext, n_prev, has_sink, after):
    t = q3.shape[1]
    nb = t // SEQ
    nc = SEQ // CHUNK
    width = bias_ext.shape[-1]
    spad = n_prev * CHUNK + SEQ + CHUNK
    return pl.pallas_call(
        _ordered(functools.partial(_attn_kernel, n_prev=n_prev, has_sink=has_sink), 4),
        grid=(nb, nc),
        in_specs=[
            pl.BlockSpec((N_KV_HEADS, CHUNK, HEAD_TILE), lambda b, c: (0, b * nc + c, 0)),
            pl.BlockSpec((1, SEQ, KV_WIDTH), lambda b, c: (k_tile, b, 0)),
            pl.BlockSpec((1, SEQ, KV_WIDTH), lambda b, c: (v_tile, b, 0)),
            pl.BlockSpec((N_HEADS, CHUNK, width), lambda b, c: (0, 0, 0), pipeline_mode=CONST_BLOCK),
            ORDER_SPEC,
        ],
        out_specs=pl.BlockSpec((N_KV_HEADS, CHUNK, HEAD_TILE), lambda b, c: (0, b * nc + c, 0)),
        out_shape=jax.ShapeDtypeStruct((N_KV_HEADS, t, HEAD_TILE), BF16),
        scratch_shapes=[pltpu.VMEM((N_KV_HEADS, spad, LANES), BF16), pltpu.VMEM((N_KV_HEADS, spad, LANES), BF16)],
        compiler_params=_params("parallel", "arbitrary"),
        name="band_attention",
    )(q3, kv3, kv3, bias_ext, after)


def _bias_ext(bias, sinks):
    fill = jnp.full((N_HEADS, CHUNK, CHUNK), NEG_INF, F32)
    if sinks is not None:
        col = lax.broadcasted_iota(jnp.int32, fill.shape, 2)
        fill = jnp.where(col == 0, sinks.astype(F32)[:, None, None] * LOG2E, fill)
    return jnp.concatenate([bias.astype(F32) * LOG2E, fill], axis=-1)


def _out_proj_kernel(a_ref, w_ref, r_ref, o_ref):
    acc = r_ref[...]
    for h in range(N_KV_HEADS):
        acc = acc + jnp.dot(a_ref[h], w_ref[h], preferred_element_type=F32)
    o_ref[...] = acc


def _out_proj(a3, w, res, row0, after):
    t = a3.shape[1]
    d = res.shape[1]
    tn = HEAD_TILE
    off = row0 // TM_PROJ
    w3 = w.reshape(N_KV_HEADS, HEAD_TILE, d)
    return pl.pallas_call(
        _ordered(_out_proj_kernel, 3),
        grid=(t // TM_PROJ, d // tn),
        in_specs=[
            pl.BlockSpec((N_KV_HEADS, TM_PROJ, HEAD_TILE), lambda i, j: (0, i, 0)),
            pl.BlockSpec((N_KV_HEADS, HEAD_TILE, tn), lambda i, j: (0, 0, j)),
            pl.BlockSpec((TM_PROJ, tn), lambda i, j: (i + off, j)),
            ORDER_SPEC,
        ],
        out_specs=pl.BlockSpec((TM_PROJ, tn), lambda i, j: (i, j)),
        out_shape=jax.ShapeDtypeStruct((t, d), F32),
        compiler_params=_params("parallel", "arbitrary"),
        name="out_proj",
    )(a3, w3, res, after)


def _router_kernel(x_ref, g_ref, whi_ref, wlo_ref, h_ref, route_ref):
    hf = _rms(x_ref[...], g_ref[...])
    hb = hf.astype(BF16)
    h_ref[...] = hb
    hlo = (hf - hb.astype(F32)).astype(BF16)
    whi = whi_ref[...]
    logits = (jnp.dot(hb, whi, preferred_element_type=F32)
              + jnp.dot(hb, wlo_ref[...], preferred_element_type=F32)
              + jnp.dot(hlo, whi, preferred_element_type=F32))
    lane = lax.broadcasted_iota(jnp.int32, logits.shape, 1).astype(F32)
    big = float(ROUTER_LANES)
    is_g = lane < N_GROUPS
    gl = jnp.where(is_g, logits, NEG_INF)
    gmax = jnp.max(gl, axis=-1, keepdims=True)
    gidx = jnp.min(jnp.where(gl == gmax, lane, big), axis=-1, keepdims=True)
    gsum = jnp.sum(jnp.where(is_g, jnp.exp(gl - gmax), 0.0), axis=-1, keepdims=True)
    g_w = 1.0 / gsum
    e0 = N_GROUPS + gidx * N_EXPERTS_PER_GROUP
    el = jnp.where((lane >= e0) & (lane < e0 + N_EXPERTS_PER_GROUP), logits, NEG_INF)
    m1 = jnp.max(el, axis=-1, keepdims=True)
    i1 = jnp.min(jnp.where(el == m1, lane, big), axis=-1, keepdims=True)
    el2 = jnp.where(lane == i1, NEG_INF, el)
    m2 = jnp.max(el2, axis=-1, keepdims=True)
    i2 = jnp.min(jnp.where(el2 == m2, lane, big), axis=-1, keepdims=True)
    ex = jnp.exp(m2 - m1)
    w1 = 1.0 / (1.0 + ex)
    w2 = ex * w1
    route = jnp.where(lane == 0, i1 - N_GROUPS,
                      jnp.where(lane == 1, i2 - N_GROUPS,
                                jnp.where(lane == 2, g_w * w1,
                                          jnp.where(lane == 3, g_w * w2, 0.0))))
    route_ref[...] = route


def _router(x, g, whi, wlo, after):
    t, d = x.shape
    return pl.pallas_call(
        _ordered(_router_kernel, 4),
        grid=(t // TM_ROW,),
        in_specs=[
            pl.BlockSpec((TM_ROW, d), lambda i: (i, 0)),
            pl.BlockSpec((1, d), lambda i: (0, 0), pipeline_mode=CONST_BLOCK),
            pl.BlockSpec((d, ROUTER_LANES), lambda i: (0, 0), pipeline_mode=CONST_BLOCK),
            pl.BlockSpec((d, ROUTER_LANES), lambda i: (0, 0), pipeline_mode=CONST_BLOCK),
            ORDER_SPEC,
        ],
        out_specs=[
            pl.BlockSpec((TM_ROW, d), lambda i: (i, 0)),
            pl.BlockSpec((TM_ROW, ROUTER_LANES), lambda i: (i, 0)),
        ],
        out_shape=[jax.ShapeDtypeStruct((t, d), BF16), jax.ShapeDtypeStruct((t, ROUTER_LANES), F32)],
        compiler_params=_params("parallel"),
        name="router",
    )(x, g.reshape(1, d), whi, wlo, after)


def _router_weights(w_rg, w_re):
    d = w_rg.shape[0]
    w = jnp.concatenate([w_rg, w_re.transpose(1, 0, 2).reshape(d, N_EXPERTS)], axis=1)
    w = jnp.pad(w, ((0, 0), (0, ROUTER_LANES - w.shape[1])))
    whi = w.astype(BF16)
    return whi, (w - whi.astype(F32)).astype(BF16)


def _experts_kernel(te_ref, nu_ref, xs_ref, cw_ref, wg_ref, wu_ref, wd_ref, y_ref):
    i = pl.program_id(0)

    @pl.when(i < nu_ref[0])
    def _():
        xs = xs_ref[...]
        a = jnp.dot(xs, wg_ref[0], preferred_element_type=F32)
        u = jnp.dot(xs, wu_ref[0], preferred_element_type=F32)
        act = a * jax.nn.sigmoid(a) * u * cw_ref[...]
        y_ref[...] = jnp.dot(act.astype(BF16), wd_ref[0], preferred_element_type=F32).astype(y_ref.dtype)

    @pl.when(i >= nu_ref[0])
    def _():
        y_ref[...] = jnp.zeros_like(y_ref)


def _experts(xs, cw, tile_expert, n_used, wg, wu, wd, layer, after):
    n_rows, d = xs.shape
    f = wg.shape[2]
    w_idx = lambda i, te, nu: (te[i] + layer * N_EXPERTS, 0, 0)
    used_idx = lambda i, te, nu: (jnp.minimum(i, nu[0] - 1), 0)
    grid_spec = pltpu.PrefetchScalarGridSpec(
        num_scalar_prefetch=2,
        grid=(n_rows // TM_EXPERT,),
        in_specs=[
            pl.BlockSpec((TM_EXPERT, d), used_idx),
            pl.BlockSpec((TM_EXPERT, 1), used_idx),
            pl.BlockSpec((1, d, f), w_idx),
            pl.BlockSpec((1, d, f), w_idx),
            pl.BlockSpec((1, f, d), w_idx),
            ORDER_SPEC,
        ],
        out_specs=pl.BlockSpec((TM_EXPERT, d), lambda i, te, nu: (i, 0)),
    )
    return pl.pallas_call(
        _ordered(_experts_kernel, 7),
        grid_spec=grid_spec,
        out_shape=jax.ShapeDtypeStruct((n_rows, d), BF16),
        compiler_params=_params("arbitrary"),
        name="experts",
    )(tile_expert, n_used, xs, cw, wg, wu, wd, after)


def _dispatch_plan(route):
    t = route.shape[0]
    n_assign = t * TOP_K
    n_fill = N_EXPERTS * TM_EXPERT
    n_rows = n_assign + n_fill
    ef = route[:, :TOP_K].T.reshape(-1).astype(jnp.int32)
    wf = route[:, TOP_K:2 * TOP_K].T.reshape(-1)
    iota = jnp.arange(n_assign, dtype=jnp.int32)
    se, order, ws = lax.sort((ef, iota, wf), num_keys=1, is_stable=True)
    edges = jnp.arange(1, N_EXPERTS + 1, dtype=jnp.int32)
    ends = jnp.sum((se[None, :] < edges[:, None]).astype(jnp.int32), axis=1)
    starts = jnp.concatenate([jnp.zeros((1,), jnp.int32), ends[:-1]])
    counts = ends - starts
    padded = (counts + TM_EXPERT - 1) // TM_EXPERT * TM_EXPERT
    pend = jnp.cumsum(padded)
    pstart = pend - padded
    shift = pstart - starts
    dshift = shift - jnp.concatenate([jnp.zeros((1,), jnp.int32), shift[:-1]])
    dest = iota + jnp.sum(jnp.where(iota[None, :] >= starts[:, None], dshift[:, None], 0), axis=0)
    jj = jnp.arange(TM_EXPERT, dtype=jnp.int32)[None, :]
    fill_key = jnp.where(jj < (padded - counts)[:, None], (pstart + counts)[:, None] + jj, n_rows).reshape(-1)
    keys = jnp.concatenate([dest, fill_key])
    toks = jnp.concatenate([order % t, jnp.arange(n_fill, dtype=jnp.int32) % t])
    wts = jnp.concatenate([ws, jnp.zeros((n_fill,), F32)])
    _, src_token, cw = lax.sort((keys, toks, wts), num_keys=1)
    _, pos = lax.sort((order, dest), num_keys=1)
    n_used = (pend[-1] // TM_EXPERT).reshape(1)
    tile_ids = jnp.arange(n_rows // TM_EXPERT, dtype=jnp.int32)
    te = jnp.sum((pend[None, :] // TM_EXPERT <= jnp.minimum(tile_ids, n_used - 1)[:, None]).astype(jnp.int32), axis=1)
    te = jnp.minimum(te, N_EXPERTS - 1)
    return src_token, cw.reshape(n_rows, 1), te, n_used, pos


def _ple_kernel(x_ref, yy_ref, p_ref, g_ref, wgd_ref, wgu_ref, wp_ref, ng_ref, o_ref, *h_refs):
    x = x_ref[...] + yy_ref[0].astype(F32) + yy_ref[1].astype(F32)
    hn = _rms(x, g_ref[...]).astype(BF16)
    tdown = jnp.dot(hn, wgd_ref[...], preferred_element_type=F32)
    gate = jax.nn.sigmoid(jnp.dot(tdown.astype(BF16), wgu_ref[...], preferred_element_type=F32))
    pp = jnp.dot(p_ref[...].astype(BF16), wp_ref[...], preferred_element_type=F32)
    out = x + gate * pp
    o_ref[...] = out
    if h_refs:
        normed = out * lax.rsqrt(jnp.mean(out * out, axis=-1, keepdims=True) + RMS_EPS)
        for k, h_ref in enumerate(h_refs):
            h_ref[...] = (normed * ng_ref[k:k + 1, :]).astype(h_ref.dtype)


def _ple(x, yy, p, p_row0, g, wgd, wgu, wp, next_gains, after):
    t, d = x.shape
    off = p_row0 // TM_PLE
    n_next = len(next_gains)
    ng = jnp.stack(next_gains) if n_next else jnp.ones((1, d), F32)
    row_spec = pl.BlockSpec((TM_PLE, d), lambda i: (i, 0))
    return pl.pallas_call(
        _ordered(_ple_kernel, 8),
        grid=(t // TM_PLE,),
        in_specs=[
            row_spec,
            pl.BlockSpec((TOP_K, TM_PLE, d), lambda i: (0, i, 0)),
            pl.BlockSpec((TM_PLE, PLE_DIM), lambda i: (i + off, 0)),
            pl.BlockSpec((1, d), lambda i: (0, 0), pipeline_mode=CONST_BLOCK),
            pl.BlockSpec((d, PLE_DIM), lambda i: (0, 0), pipeline_mode=CONST_BLOCK),
            pl.BlockSpec((PLE_DIM, d), lambda i: (0, 0), pipeline_mode=CONST_BLOCK),
            pl.BlockSpec((PLE_DIM, d), lambda i: (0, 0), pipeline_mode=CONST_BLOCK),
            pl.BlockSpec(ng.shape, lambda i: (0, 0), pipeline_mode=CONST_BLOCK),
            ORDER_SPEC,
        ],
        out_specs=[row_spec] * (1 + n_next),
        out_shape=[jax.ShapeDtypeStruct((t, d), F32)] + [jax.ShapeDtypeStruct((t, d), BF16)] * n_next,
        compiler_params=_params("parallel"),
        name="ple",
    )(x, yy, p, g.reshape(1, d), wgd, wgu, wp, ng, after)


def _t5_bucket(rel):
    nb = N_BUCKETS // 2
    n = -rel
    ret = np.where(n < 0, nb, 0)
    n = np.abs(n)
    max_exact = nb // 2
    large = max_exact + (np.log(np.maximum(n, 1) / max_exact)
                         / np.log(T5_MAX_DIST / max_exact) * (nb - max_exact)).astype(np.int32)
    large = np.minimum(large, nb - 1)
    return (ret + np.where(n < max_exact, n, large)).astype(np.int32)


def _clipped_distance(rel):
    return np.clip(-rel, -REL_CLIP, REL_CLIP) + REL_CLIP


def _band_bias(table, rel_to_index, n_prev):
    band = (n_prev + 1) * CHUNK
    rel = np.arange(band + CHUNK - 1) - (CHUNK - 1) - n_prev * CHUNK
    ext = table.astype(F32)[:, rel_to_index(rel)]
    return jnp.stack([ext[:, CHUNK - 1 - q:CHUNK - 1 - q + band] for q in range(CHUNK)], axis=1)


def _head_gain(g, n_heads, scale=1.0):
    return jnp.tile(g.astype(F32) * scale, n_heads)


def kernel(x, p, t5_bias, attn_norm_a, w_qkv_a, q_norm_a, k_norm_a, sinks_a, w_o_a, kv_norm_b, w_kv_b, k_norm_b, attn_norm_b, w_q_b, q_norm_b, rel_bias_b, w_o_b, ffn_norm, w_router_group, w_router_expert, w_exp_gate, w_exp_up, w_exp_down, w_ple_proj, ple_norm, w_ple_gate_down, w_ple_gate_up):
    b, s, d = x.shape
    t = b * s
    ts = t // N_STREAMS
    f = w_exp_gate.shape[-1]
    x_in = x.reshape(t, d)
    p_flat = p.reshape(DEPTH * t, PLE_DIM)
    q_scale = HEAD_DIM ** -0.5 * LOG2E
    bias_a = _band_bias(t5_bias, _t5_bucket, WIN_CHUNKS)
    ones_kv = jnp.ones((KV_WIDTH,), F32)
    wg_all = w_exp_gate.astype(BF16).reshape(DEPTH * N_EXPERTS, d, f)
    wu_all = w_exp_up.astype(BF16).reshape(DEPTH * N_EXPERTS, d, f)
    wd_all = w_exp_down.astype(BF16).reshape(DEPTH * N_EXPERTS, f, d)
    streams = range(N_STREAMS)
    xs_ = [None] * N_STREAMS
    hs_ = [_rmsnorm(x_in, attn_norm_a[0], k * ts, ts) for k in streams]
    kv_b = [None] * N_STREAMS
    last = hs_[-1]
    for i in range(DEPTH):
        is_a = i < N_A_LAYERS
        j = i - N_A_LAYERS
        if is_a:
            gains = jnp.concatenate([_head_gain(q_norm_a[i], N_HEADS, q_scale),
                                     _head_gain(k_norm_a[i], N_KV_HEADS), ones_kv])
            w_in, n_norm = w_qkv_a[i].astype(BF16), N_KV_HEADS + 1
            bias_ext = _bias_ext(bias_a, sinks_a[i])
            w_out = w_o_a[i].astype(BF16)
        else:
            gains = _head_gain(q_norm_b[j], N_HEADS, q_scale)
            w_in, n_norm = w_q_b[j].astype(BF16), N_KV_HEADS
            bias_ext = _bias_ext(_band_bias(rel_bias_b[j], _clipped_distance, B_PREV_CHUNKS), None)
            w_out = w_o_b[j].astype(BF16)
        whi, wlo = _router_weights(w_router_group[i], w_router_expert[i])
        wgd, wgu, wpp = (w_ple_gate_down[i].astype(BF16), w_ple_gate_up[i].astype(BF16),
                         w_ple_proj[i].astype(BF16))
        if i + 1 < DEPTH:
            next_gains = [attn_norm_a[i + 1] if i + 1 < N_A_LAYERS else attn_norm_b[i + 1 - N_A_LAYERS]]
            if i == N_A_LAYERS - 1:
                next_gains.append(kv_norm_b)
        else:
            next_gains = []

        q3 = [None] * N_STREAMS
        for k in streams:
            q3[k] = last = _proj(hs_[k], w_in, gains, n_norm, last)
        o3 = [None] * N_STREAMS
        for k in streams:
            if is_a:
                o3[k] = _attention(q3[k], q3[k], N_KV_HEADS, N_KV_HEADS + 1, bias_ext, WIN_CHUNKS, True, last)
            else:
                o3[k] = _attention(q3[k], kv_b[k], 0, 1, bias_ext, B_PREV_CHUNKS, False, last)
            last = o3[k]
        routed = [None] * N_STREAMS
        for k in streams:
            res, row0 = (x_in, k * ts) if i == 0 else (xs_[k], 0)
            xs_[k] = _out_proj(o3[k], w_out, res, row0, last)
            hk, route = _router(xs_[k], ffn_norm[i], whi, wlo, xs_[k])
            routed[k] = (hk,) + _dispatch_plan(route)
            last = routed[k][1]
        ys = [None] * N_STREAMS
        for k in streams:
            hk, src_token, cw, tile_expert, n_used, pos = routed[k]
            y = last = _experts(_rows(hk, src_token), cw, tile_expert, n_used, wg_all, wu_all, wd_all, i, last)
            ys[k] = _rows(y, pos).reshape(TOP_K, ts, d)
        for k in streams:
            outs = _ple(xs_[k], ys[k], p_flat, i * t + k * ts, ple_norm[i], wgd, wgu, wpp, next_gains, last)
            xs_[k] = last = outs[0]
            if next_gains:
                hs_[k] = outs[1]
            if i == N_A_LAYERS - 1:
                kv_gain = jnp.concatenate([_head_gain(k_norm_b, N_KV_HEADS), ones_kv])
                kv_b[k] = last = _proj(outs[2], w_kv_b.astype(BF16), kv_gain, 1, last)
    return jnp.concatenate(xs_, axis=0).reshape(b, s, d)
```

```python
import functools
import math

import numpy as np
import jax
import jax.numpy as jnp
from jax import lax
from jax.experimental import pallas as pl
from jax.experimental.pallas import tpu as pltpu

D_MODEL = 4096
BATCH = 8
SEQ = 2048
DEPTH = 4
CHUNK = 64
N_A_LAYERS = DEPTH // 2
HEAD_DIM = 64
N_HEADS = D_MODEL // HEAD_DIM
N_KV_HEADS = 8
KV_GROUP = N_HEADS // N_KV_HEADS
KV_WIDTH = N_KV_HEADS * HEAD_DIM
WIN_CHUNKS = 2
B_PREV_CHUNKS = 8
N_BUCKETS = 32
T5_MAX_DIST = 128
REL_CLIP = 128
N_GROUPS = 4
N_EXPERTS_PER_GROUP = 8
N_EXPERTS = N_GROUPS * N_EXPERTS_PER_GROUP
EXPERT_FF = 384
TOP_K = 2
PLE_DIM = 256
RMS_EPS = 1e-6
NEG_INF = -1e30
LOG2E = math.log2(math.e)

LANES = 128
MXU_TILE = 256
HEAD_TILE = KV_GROUP * HEAD_DIM
ROUTER_LANES = LANES
VMEM_LIMIT = 56 * 1024 * 1024

N_STREAMS = 2
TM_PROJ = 1024
SUB_ROWS = 256
TM_ROW = 512
TM_PLE = 256
TM_EXPERT = 128

F32 = jnp.float32
BF16 = jnp.bfloat16
CONST_BLOCK = pl.Buffered(1)


def _params(*sem):
    return pltpu.CompilerParams(dimension_semantics=sem, vmem_limit_bytes=VMEM_LIMIT)


def _rms(x, g):
    return x * lax.rsqrt(jnp.mean(x * x, axis=-1, keepdims=True) + RMS_EPS) * g


def _rows(table, idx):
    return table.at[idx].get(mode="promise_in_bounds")


ORDER_SPEC = pl.BlockSpec(memory_space=pl.ANY)


def _ordered(kernel_fn, n_in):
    def ordered_kernel(*refs):
        return kernel_fn(*refs[:n_in], *refs[n_in + 1:])
    return ordered_kernel


def _rmsnorm_kernel(x_ref, g_ref, h_ref):
    h_ref[...] = _rms(x_ref[...], g_ref[...]).astype(h_ref.dtype)


def _rmsnorm(x, g, row0, n_rows):
    d = x.shape[1]
    off = row0 // TM_ROW
    return pl.pallas_call(
        _rmsnorm_kernel,
        grid=(n_rows // TM_ROW,),
        in_specs=[pl.BlockSpec((TM_ROW, d), lambda i: (i + off, 0)),
                  pl.BlockSpec((1, d), lambda i: (0, 0), pipeline_mode=CONST_BLOCK)],
        out_specs=pl.BlockSpec((TM_ROW, d), lambda i: (i, 0)),
        out_shape=jax.ShapeDtypeStruct((n_rows, d), BF16),
        compiler_params=_params("parallel"),
        name="rmsnorm",
    )(x, g.reshape(1, d))


def _proj_kernel(h_ref, w_ref, cg_ref, bd_ref, o_ref, *, n_norm_tiles, n_col_tiles):
    j = pl.program_id(1)
    tm, tn = h_ref.shape[0], w_ref.shape[1]

    def tile(normed):
        for r in range(tm // SUB_ROWS):
            rows = pl.ds(r * SUB_ROWS, SUB_ROWS)
            acc = jnp.dot(h_ref[rows, :], w_ref[...], preferred_element_type=F32)
            if normed:
                sq = (acc * acc).astype(BF16)
                ms = jnp.concatenate(
                    [jnp.dot(sq[:, c * MXU_TILE:(c + 1) * MXU_TILE], bd_ref[...], preferred_element_type=F32)
                     for c in range(tn // MXU_TILE)], axis=1)
                acc = acc * lax.rsqrt(ms + RMS_EPS) * cg_ref[...]
            o_ref[0, rows, :] = acc.astype(o_ref.dtype)

    if n_norm_tiles > 0:
        pl.when(j < n_norm_tiles)(lambda: tile(True))
    if n_norm_tiles < n_col_tiles:
        pl.when(j >= n_norm_tiles)(lambda: tile(False))


def _proj(h, w, col_gain, n_norm_tiles, after):
    t, d = h.shape
    n = w.shape[1]
    tn = HEAD_TILE
    bd = jnp.asarray(np.kron(np.eye(MXU_TILE // HEAD_DIM), np.full((HEAD_DIM, HEAD_DIM), 1.0 / HEAD_DIM)), BF16)
    return pl.pallas_call(
        _ordered(functools.partial(_proj_kernel, n_norm_tiles=n_norm_tiles, n_col_tiles=n // tn), 4),
        grid=(t // TM_PROJ, n // tn),
        in_specs=[
            pl.BlockSpec((TM_PROJ, d), lambda i, j: (i, 0)),
            pl.BlockSpec((d, tn), lambda i, j: (0, j)),
            pl.BlockSpec((1, tn), lambda i, j: (0, j)),
            pl.BlockSpec((MXU_TILE, MXU_TILE), lambda i, j: (0, 0), pipeline_mode=CONST_BLOCK),
            ORDER_SPEC,
        ],
        out_specs=pl.BlockSpec((1, TM_PROJ, tn), lambda i, j: (j, i, 0)),
        out_shape=jax.ShapeDtypeStruct((n // tn, t, tn), BF16),
        compiler_params=_params("parallel", "arbitrary"),
        name="proj",
    )(h, w, col_gain.reshape(1, n), bd, after)


def _attn_kernel(q_ref, kin_ref, vin_ref, bias_ref, o_ref, k_scr, v_scr, *, n_prev, has_sink):
    c = pl.program_id(1)
    band = (n_prev + 1) * CHUNK
    pad = n_prev * CHUNK
    width = bias_ref.shape[-1]
    lo = lax.broadcasted_iota(jnp.int32, (CHUNK, LANES), 1) < HEAD_DIM

    @pl.when(c == 0)
    def _():
        lo_s = lax.broadcasted_iota(jnp.int32, (SEQ, LANES), 1) < HEAD_DIM
        for src, dst in ((kin_ref, k_scr), (vin_ref, v_scr)):
            dst[:, :pad, :] = jnp.zeros((N_KV_HEADS, pad, LANES), dst.dtype)
            dst[:, pad + SEQ:, :] = jnp.zeros((N_KV_HEADS, CHUNK, LANES), dst.dtype)
            for hp in range(N_KV_HEADS // 2):
                two = src[0, :, hp * LANES:(hp + 1) * LANES].astype(F32)
                swapped = pltpu.roll(two, HEAD_DIM, axis=1)
                dst[2 * hp, pad:pad + SEQ, :] = jnp.where(lo_s, two, swapped).astype(dst.dtype)
                dst[2 * hp + 1, pad:pad + SEQ, :] = jnp.where(lo_s, swapped, two).astype(dst.dtype)

    row0 = pl.multiple_of(c * CHUNK, CHUNK)
    kslot = lax.broadcasted_iota(jnp.int32, (1, width), 1)
    first_valid = (n_prev - c) * CHUNK
    keep = (kslot >= first_valid) & (kslot < band)
    pairs = KV_GROUP // 2
    def one_kv_head(h):
        qh = q_ref[h]
        zero = jnp.zeros((CHUNK, LANES), qh.dtype)
        parts = []
        for gg in range(pairs):
            qp = qh[:, gg * LANES:(gg + 1) * LANES]
            parts.append(jnp.where(lo, qp, zero))
            parts.append(jnp.where(lo, zero, qp))
        lhs = jnp.concatenate(parts, axis=0)
        k2 = k_scr[h, pl.ds(row0, width), :]
        v2 = v_scr[h, pl.ds(row0, width), :]
        s = lax.dot_general(lhs, k2, (((1,), (1,)), ((), ())), preferred_element_type=F32)
        bias = bias_ref[h * KV_GROUP:(h + 1) * KV_GROUP].reshape(KV_GROUP * CHUNK, width)
        if has_sink:
            s = jnp.where(keep, s, 0.0) + jnp.where(kslot < first_valid, NEG_INF, bias)
        else:
            s = jnp.where(keep, s + bias, NEG_INF)
        m = jnp.max(s, axis=-1, keepdims=True)
        p = jnp.exp2(s - m)
        l = jnp.sum(p, axis=-1, keepdims=True)
        if has_sink:
            tail = jnp.where(kslot[:, width - LANES:] < band, p[:, width - LANES:], 0.0)
            p = jnp.concatenate([p[:, :width - LANES], tail], axis=1)
        o2 = jnp.dot(p.astype(BF16), v2, preferred_element_type=F32) * (1.0 / l)
        outs = []
        for gg in range(pairs):
            a = o2[(2 * gg) * CHUNK:(2 * gg + 1) * CHUNK]
            b = o2[(2 * gg + 1) * CHUNK:(2 * gg + 2) * CHUNK]
            outs.append(jnp.where(lo, a, b))
        o_ref[h] = jnp.concatenate(outs, axis=1).astype(o_ref.dtype)

    for h in range(N_KV_HEADS):
        one_kv_head(h)


def _attention(q3, kv3, k_tile, v_tile, bias_ext, n_prev, has_sink, after):
    t = q3.shape[1]
    nb = t // SEQ
    nc = SEQ // CHUNK
    width = bias_ext.shape[-1]
    spad = n_prev * CHUNK + SEQ + CHUNK
    return pl.pallas_call(
        _ordered(functools.partial(_attn_kernel, n_prev=n_prev, has_sink=has_sink), 4),
        grid=(nb, nc),
        in_specs=[
            pl.BlockSpec((N_KV_HEADS, CHUNK, HEAD_TILE), lambda b, c: (0, b * nc + c, 0)),
            pl.BlockSpec((1, SEQ, KV_WIDTH), lambda b, c: (k_tile, b, 0)),
            pl.BlockSpec((1, SEQ, KV_WIDTH), lambda b, c: (v_tile, b, 0)),
            pl.BlockSpec((N_HEADS, CHUNK, width), lambda b, c: (0, 0, 0), pipeline_mode=CONST_BLOCK),
            ORDER_SPEC,
        ],
        out_specs=pl.BlockSpec((N_KV_HEADS, CHUNK, HEAD_TILE), lambda b, c: (0, b * nc + c, 0)),
        out_shape=jax.ShapeDtypeStruct((N_KV_HEADS, t, HEAD_TILE), BF16),
        scratch_shapes=[pltpu.VMEM((N_KV_HEADS, spad, LANES), BF16), pltpu.VMEM((N_KV_HEADS, spad, LANES), BF16)],
        compiler_params=_params("parallel", "arbitrary"),
        name="band_attention",
    )(q3, kv3, kv3, bias_ext, after)


def _bias_ext(bias, sinks):
    fill = jnp.full((N_HEADS, CHUNK, CHUNK), NEG_INF, F32)
    if sinks is not None:
        col = lax.broadcasted_iota(jnp.int32, fill.shape, 2)
        fill = jnp.where(col == 0, sinks.astype(F32)[:, None, None] * LOG2E, fill)
    return jnp.concatenate([bias.astype(F32) * LOG2E, fill], axis=-1)


def _out_proj_kernel(a_ref, w_ref, r_ref, o_ref):
    acc = r_ref[...]
    for h in range(N_KV_HEADS):
        acc = acc + jnp.dot(a_ref[h], w_ref[h], preferred_element_type=F32)
    o_ref[...] = acc


def _out_proj(a3, w, res, row0, after):
    t = a3.shape[1]
    d = res.shape[1]
    tn = HEAD_TILE
    off = row0 // TM_PROJ
    w3 = w.reshape(N_KV_HEADS, HEAD_TILE, d)
    return pl.pallas_call(
        _ordered(_out_proj_kernel, 3),
        grid=(t // TM_PROJ, d // tn),
        in_specs=[
            pl.BlockSpec((N_KV_HEADS, TM_PROJ, HEAD_TILE), lambda i, j: (0, i, 0)),
            pl.BlockSpec((N_KV_HEADS, HEAD_TILE, tn), lambda i, j: (0, 0, j)),
            pl.BlockSpec((TM_PROJ, tn), lambda i, j: (i + off, j)),
            ORDER_SPEC,
        ],
        out_specs=pl.BlockSpec((TM_PROJ, tn), lambda i, j: (i, j)),
        out_shape=jax.ShapeDtypeStruct((t, d), F32),
        compiler_params=_params("parallel", "arbitrary"),
        name="out_proj",
    )(a3, w3, res, after)


def _router_kernel(x_ref, g_ref, w_ref, h_ref, route_ref):
    hf = _rms(x_ref[...], g_ref[...])
    hb = hf.astype(BF16)
    h_ref[...] = hb
    hlo = (hf - hb.astype(F32)).astype(BF16)
    both = jnp.dot(hb, w_ref[...], preferred_element_type=F32)
    logits = (both[:, :ROUTER_LANES] + both[:, ROUTER_LANES:]
              + jnp.dot(hlo, w_ref[:, :ROUTER_LANES], preferred_element_type=F32))
    lane = lax.broadcasted_iota(jnp.int32, logits.shape, 1).astype(F32)
    big = float(ROUTER_LANES)
    is_g = lane < N_GROUPS
    gl = jnp.where(is_g, logits, NEG_INF)
    gmax = jnp.max(gl, axis=-1, keepdims=True)
    gidx = jnp.min(jnp.where(gl == gmax, lane, big), axis=-1, keepdims=True)
    gsum = jnp.sum(jnp.where(is_g, jnp.exp(gl - gmax), 0.0), axis=-1, keepdims=True)
    g_w = 1.0 / gsum
    e0 = N_GROUPS + gidx * N_EXPERTS_PER_GROUP
    el = jnp.where((lane >= e0) & (lane < e0 + N_EXPERTS_PER_GROUP), logits, NEG_INF)
    m1 = jnp.max(el, axis=-1, keepdims=True)
    i1 = jnp.min(jnp.where(el == m1, lane, big), axis=-1, keepdims=True)
    el2 = jnp.where(lane == i1, NEG_INF, el)
    m2 = jnp.max(el2, axis=-1, keepdims=True)
    i2 = jnp.min(jnp.where(el2 == m2, lane, big), axis=-1, keepdims=True)
    ex = jnp.exp(m2 - m1)
    w1 = 1.0 / (1.0 + ex)
    w2 = ex * w1
    route = jnp.where(lane == 0, i1 - N_GROUPS,
                      jnp.where(lane == 1, i2 - N_GROUPS,
                                jnp.where(lane == 2, g_w * w1,
                                          jnp.where(lane == 3, g_w * w2, 0.0))))
    route_ref[...] = route


def _router(x, g, w_hi_lo, after):
    t, d = x.shape
    return pl.pallas_call(
        _ordered(_router_kernel, 3),
        grid=(t // TM_ROW,),
        in_specs=[
            pl.BlockSpec((TM_ROW, d), lambda i: (i, 0)),
            pl.BlockSpec((1, d), lambda i: (0, 0), pipeline_mode=CONST_BLOCK),
            pl.BlockSpec((d, 2 * ROUTER_LANES), lambda i: (0, 0), pipeline_mode=CONST_BLOCK),
            ORDER_SPEC,
        ],
        out_specs=[
            pl.BlockSpec((TM_ROW, d), lambda i: (i, 0)),
            pl.BlockSpec((TM_ROW, ROUTER_LANES), lambda i: (i, 0)),
        ],
        out_shape=[jax.ShapeDtypeStruct((t, d), BF16), jax.ShapeDtypeStruct((t, ROUTER_LANES), F32)],
        compiler_params=_params("parallel"),
        name="router",
    )(x, g.reshape(1, d), w_hi_lo, after)


def _router_weights(w_rg, w_re):
    d = w_rg.shape[0]
    w = jnp.concatenate([w_rg, w_re.transpose(1, 0, 2).reshape(d, N_EXPERTS)], axis=1)
    w = jnp.pad(w, ((0, 0), (0, ROUTER_LANES - w.shape[1])))
    whi = w.astype(BF16)
    return jnp.concatenate([whi, (w - whi.astype(F32)).astype(BF16)], axis=1)


def _experts_kernel(te_ref, nu_ref, xs_ref, cw_ref, wg_ref, wu_ref, wd_ref, y_ref):
    i = pl.program_id(0)

    @pl.when(i < nu_ref[0])
    def _():
        xs = xs_ref[...]
        a = jnp.dot(xs, wg_ref[0], preferred_element_type=F32)
        u = jnp.dot(xs, wu_ref[0], preferred_element_type=F32)
        act = a * jax.nn.sigmoid(a) * u * cw_ref[...]
        y_ref[...] = jnp.dot(act.astype(BF16), wd_ref[0], preferred_element_type=F32).astype(y_ref.dtype)

    @pl.when(i >= nu_ref[0])
    def _():
        y_ref[...] = jnp.zeros_like(y_ref)


def _experts(xs, cw, tile_expert, n_used, wg, wu, wd, layer, after):
    n_rows, d = xs.shape
    f = wg.shape[2]
    w_idx = lambda i, te, nu: (te[i] + layer * N_EXPERTS, 0, 0)
    used_idx = lambda i, te, nu: (jnp.minimum(i, nu[0] - 1), 0)
    grid_spec = pltpu.PrefetchScalarGridSpec(
        num_scalar_prefetch=2,
        grid=(n_rows // TM_EXPERT,),
        in_specs=[
            pl.BlockSpec((TM_EXPERT, d), used_idx),
            pl.BlockSpec((TM_EXPERT, 1), used_idx),
            pl.BlockSpec((1, d, f), w_idx),
            pl.BlockSpec((1, d, f), w_idx),
            pl.BlockSpec((1, f, d), w_idx),
            ORDER_SPEC,
        ],
        out_specs=pl.BlockSpec((TM_EXPERT, d), lambda i, te, nu: (i, 0)),
    )
    return pl.pallas_call(
        _ordered(_experts_kernel, 7),
        grid_spec=grid_spec,
        out_shape=jax.ShapeDtypeStruct((n_rows, d), BF16),
        compiler_params=_params("arbitrary"),
        name="experts",
    )(tile_expert, n_used, xs, cw, wg, wu, wd, after)


def _dispatch_plan(route):
    t = route.shape[0]
    n_assign = t * TOP_K
    n_fill = N_EXPERTS * TM_EXPERT
    n_rows = n_assign + n_fill
    ef = route[:, :TOP_K].T.reshape(-1).astype(jnp.int32)
    wf = route[:, TOP_K:2 * TOP_K].T.reshape(-1)
    iota = jnp.arange(n_assign, dtype=jnp.int32)
    se, order, ws = lax.sort((ef, iota, wf), num_keys=1, is_stable=True)
    edges = jnp.arange(1, N_EXPERTS + 1, dtype=jnp.int32)
    ends = jnp.sum((se[None, :] < edges[:, None]).astype(jnp.int32), axis=1)
    starts = jnp.concatenate([jnp.zeros((1,), jnp.int32), ends[:-1]])
    counts = ends - starts
    padded = (counts + TM_EXPERT - 1) // TM_EXPERT * TM_EXPERT
    pend = jnp.cumsum(padded)
    pstart = pend - padded
    shift = pstart - starts
    dshift = shift - jnp.concatenate([jnp.zeros((1,), jnp.int32), shift[:-1]])
    dest = iota + jnp.sum(jnp.where(iota[None, :] >= starts[:, None], dshift[:, None], 0), axis=0)
    jj = jnp.arange(TM_EXPERT, dtype=jnp.int32)[None, :]
    fill_key = jnp.where(jj < (padded - counts)[:, None], (pstart + counts)[:, None] + jj, n_rows).reshape(-1)
    keys = jnp.concatenate([dest, fill_key])
    toks = jnp.concatenate([order % t, jnp.arange(n_fill, dtype=jnp.int32) % t])
    wts = jnp.concatenate([ws, jnp.zeros((n_fill,), F32)])
    _, src_token, cw = lax.sort((keys, toks, wts), num_keys=1)
    _, pos = lax.sort((order, dest), num_keys=1)
    n_used = (pend[-1] // TM_EXPERT).reshape(1)
    tile_ids = jnp.arange(n_rows // TM_EXPERT, dtype=jnp.int32)
    te = jnp.sum((pend[None, :] // TM_EXPERT <= jnp.minimum(tile_ids, n_used - 1)[:, None]).astype(jnp.int32), axis=1)
    te = jnp.minimum(te, N_EXPERTS - 1)
    return src_token, cw.reshape(n_rows, 1), te, n_used, pos


def _ple_kernel(x_ref, yy_ref, p_ref, g_ref, wgd_ref, wgu_ref, wp_ref, ng_ref, o_ref, *h_refs):
    x = x_ref[...] + yy_ref[0].astype(F32) + yy_ref[1].astype(F32)
    hn = _rms(x, g_ref[...]).astype(BF16)
    tdown = jnp.dot(hn, wgd_ref[...], preferred_element_type=F32)
    gate = jax.nn.sigmoid(jnp.dot(tdown.astype(BF16), wgu_ref[...], preferred_element_type=F32))
    pp = jnp.dot(p_ref[...].astype(BF16), wp_ref[...], preferred_element_type=F32)
    out = x + gate * pp
    o_ref[...] = out
    if h_refs:
        normed = out * lax.rsqrt(jnp.mean(out * out, axis=-1, keepdims=True) + RMS_EPS)
        for k, h_ref in enumerate(h_refs):
            h_ref[...] = (normed * ng_ref[k:k + 1, :]).astype(h_ref.dtype)


def _ple(x, yy, p, p_row0, g, wgd, wgu, wp, next_gains, after):
    t, d = x.shape
    off = p_row0 // TM_PLE
    n_next = len(next_gains)
    ng = jnp.stack(next_gains) if n_next else jnp.ones((1, d), F32)
    row_spec = pl.BlockSpec((TM_PLE, d), lambda i: (i, 0))
    return pl.pallas_call(
        _ordered(_ple_kernel, 8),
        grid=(t // TM_PLE,),
        in_specs=[
            row_spec,
            pl.BlockSpec((TOP_K, TM_PLE, d), lambda i: (0, i, 0)),
            pl.BlockSpec((TM_PLE, PLE_DIM), lambda i: (i + off, 0)),
            pl.BlockSpec((1, d), lambda i: (0, 0), pipeline_mode=CONST_BLOCK),
            pl.BlockSpec((d, PLE_DIM), lambda i: (0, 0), pipeline_mode=CONST_BLOCK),
            pl.BlockSpec((PLE_DIM, d), lambda i: (0, 0), pipeline_mode=CONST_BLOCK),
            pl.BlockSpec((PLE_DIM, d), lambda i: (0, 0), pipeline_mode=CONST_BLOCK),
            pl.BlockSpec(ng.shape, lambda i: (0, 0), pipeline_mode=CONST_BLOCK),
            ORDER_SPEC,
        ],
        out_specs=[row_spec] * (1 + n_next),
        out_shape=[jax.ShapeDtypeStruct((t, d), F32)] + [jax.ShapeDtypeStruct((t, d), BF16)] * n_next,
        compiler_params=_params("parallel"),
        name="ple",
    )(x, yy, p, g.reshape(1, d), wgd, wgu, wp, ng, after)


def _t5_bucket(rel):
    nb = N_BUCKETS // 2
    n = -rel
    ret = np.where(n < 0, nb, 0)
    n = np.abs(n)
    max_exact = nb // 2
    large = max_exact + (np.log(np.maximum(n, 1) / max_exact)
                         / np.log(T5_MAX_DIST / max_exact) * (nb - max_exact)).astype(np.int32)
    large = np.minimum(large, nb - 1)
    return (ret + np.where(n < max_exact, n, large)).astype(np.int32)


def _clipped_distance(rel):
    return np.clip(-rel, -REL_CLIP, REL_CLIP) + REL_CLIP


def _band_bias(table, rel_to_index, n_prev):
    band = (n_prev + 1) * CHUNK
    rel = np.arange(band + CHUNK - 1) - (CHUNK - 1) - n_prev * CHUNK
    ext = table.astype(F32)[:, rel_to_index(rel)]
    return jnp.stack([ext[:, CHUNK - 1 - q:CHUNK - 1 - q + band] for q in range(CHUNK)], axis=1)


def _head_gain(g, n_heads, scale=1.0):
    return jnp.tile(g.astype(F32) * scale, n_heads)


def kernel(x, p, t5_bias, attn_norm_a, w_qkv_a, q_norm_a, k_norm_a, sinks_a, w_o_a, kv_norm_b, w_kv_b, k_norm_b, attn_norm_b, w_q_b, q_norm_b, rel_bias_b, w_o_b, ffn_norm, w_router_group, w_router_expert, w_exp_gate, w_exp_up, w_exp_down, w_ple_proj, ple_norm, w_ple_gate_down, w_ple_gate_up):
    b, s, d = x.shape
    t = b * s
    ts = t // N_STREAMS
    f = w_exp_gate.shape[-1]
    x_in = x.reshape(t, d)
    p_flat = p.reshape(DEPTH * t, PLE_DIM)
    q_scale = HEAD_DIM ** -0.5 * LOG2E
    bias_a = _band_bias(t5_bias, _t5_bucket, WIN_CHUNKS)
    ones_kv = jnp.ones((KV_WIDTH,), F32)
    wg_all = w_exp_gate.astype(BF16).reshape(DEPTH * N_EXPERTS, d, f)
    wu_all = w_exp_up.astype(BF16).reshape(DEPTH * N_EXPERTS, d, f)
    wd_all = w_exp_down.astype(BF16).reshape(DEPTH * N_EXPERTS, f, d)
    streams = range(N_STREAMS)
    xs_ = [None] * N_STREAMS
    hs_ = [_rmsnorm(x_in, attn_norm_a[0], k * ts, ts) for k in streams]
    kv_b = [None] * N_STREAMS
    last = hs_[-1]

    def proj_weights(i):
        if i < N_A_LAYERS:
            gains = jnp.concatenate([_head_gain(q_norm_a[i], N_HEADS, q_scale),
                                     _head_gain(k_norm_a[i], N_KV_HEADS), ones_kv])
            return w_qkv_a[i].astype(BF16), gains, N_KV_HEADS + 1
        j = i - N_A_LAYERS
        return w_q_b[j].astype(BF16), _head_gain(q_norm_b[j], N_HEADS, q_scale), N_KV_HEADS

    lead, trail = 0, 1
    q3 = [None] * N_STREAMS
    q3[lead] = last = _proj(hs_[lead], *proj_weights(0), last)
    for i in range(DEPTH):
        is_a = i < N_A_LAYERS
        j = i - N_A_LAYERS
        if is_a:
            bias_ext = _bias_ext(bias_a, sinks_a[i])
            w_out = w_o_a[i].astype(BF16)
        else:
            bias_ext = _bias_ext(_band_bias(rel_bias_b[j], _clipped_distance, B_PREV_CHUNKS), None)
            w_out = w_o_b[j].astype(BF16)
        w_router = _router_weights(w_router_group[i], w_router_expert[i])
        wgd, wgu, wpp = (w_ple_gate_down[i].astype(BF16), w_ple_gate_up[i].astype(BF16),
                         w_ple_proj[i].astype(BF16))
        if i + 1 < DEPTH:
            next_gains = [attn_norm_a[i + 1] if i + 1 < N_A_LAYERS else attn_norm_b[i + 1 - N_A_LAYERS]]
            if i == N_A_LAYERS - 1:
                next_gains.append(kv_norm_b)
        else:
            next_gains = []

        def attend(k, after):
            if is_a:
                return _attention(q3[k], q3[k], N_KV_HEADS, N_KV_HEADS + 1, bias_ext, WIN_CHUNKS, True, after)
            return _attention(q3[k], kv_b[k], 0, 1, bias_ext, B_PREV_CHUNKS, False, after)

        def route(k, o3, after):
            res, row0 = (x_in, k * ts) if i == 0 else (xs_[k], 0)
            xs_[k] = _out_proj(o3, w_out, res, row0, after)
            hk, rt = _router(xs_[k], ffn_norm[i], w_router, xs_[k])
            return (hk,) + _dispatch_plan(rt)

        def experts(k, routed, after):
            hk, src_token, cw, tile_expert, n_used, pos = routed
            y = _experts(_rows(hk, src_token), cw, tile_expert, n_used, wg_all, wu_all, wd_all, i, after)
            return y, pos

        def finish(k, y, pos, after):
            yy = _rows(y, pos).reshape(TOP_K, ts, d)
            outs = _ple(xs_[k], yy, p_flat, i * t + k * ts, ple_norm[i], wgd, wgu, wpp, next_gains, after)
            xs_[k] = tail = outs[0]
            if next_gains:
                hs_[k] = outs[1]
            if i == N_A_LAYERS - 1:
                kv_gain = jnp.concatenate([_head_gain(k_norm_b, N_KV_HEADS), ones_kv])
                kv_b[k] = tail = _proj(outs[2], w_kv_b.astype(BF16), kv_gain, 1, tail)
            return tail

        o3_lead = last = attend(lead, last)
        q3[trail] = last = _proj(hs_[trail], *proj_weights(i), last)
        routed_lead = route(lead, o3_lead, last)
        o3_trail = last = attend(trail, routed_lead[1])
        routed_trail = route(trail, o3_trail, last)
        y_lead, pos_lead = experts(lead, routed_lead, routed_trail[1])
        y_trail, pos_trail = experts(trail, routed_trail, y_lead)
        last = finish(lead, y_lead, pos_lead, y_trail)
        if i + 1 < DEPTH:
            q3[lead] = last = _proj(hs_[lead], *proj_weights(i + 1), last)
        last = finish(trail, y_trail, pos_trail, last)
    return jnp.concatenate(xs_, axis=0).reshape(b, s, d)
```

```python
import functools
import math

import numpy as np
import jax
import jax.numpy as jnp
from jax import lax
from jax.experimental import pallas as pl
from jax.experimental.pallas import tpu as pltpu

D_MODEL = 4096
BATCH = 8
SEQ = 2048
DEPTH = 4
CHUNK = 64
N_A_LAYERS = DEPTH // 2
HEAD_DIM = 64
N_HEADS = D_MODEL // HEAD_DIM
N_KV_HEADS = 8
KV_GROUP = N_HEADS // N_KV_HEADS
KV_WIDTH = N_KV_HEADS * HEAD_DIM
WIN_CHUNKS = 2
B_PREV_CHUNKS = 8
N_BUCKETS = 32
T5_MAX_DIST = 128
REL_CLIP = 128
N_GROUPS = 4
N_EXPERTS_PER_GROUP = 8
N_EXPERTS = N_GROUPS * N_EXPERTS_PER_GROUP
EXPERT_FF = 384
TOP_K = 2
PLE_DIM = 256
RMS_EPS = 1e-6
NEG_INF = -1e30
LOG2E = math.log2(math.e)

LANES = 128
MXU_TILE = 256
HEAD_TILE = KV_GROUP * HEAD_DIM
ROUTER_LANES = LANES
VMEM_LIMIT = 56 * 1024 * 1024

N_STREAMS = 2
TM_PROJ = 1024
SUB_ROWS = 256
TM_ROW = 512
TM_PLE = 256
TM_EXPERT = 256

F32 = jnp.float32
BF16 = jnp.bfloat16
CONST_BLOCK = pl.Buffered(1)


def _params(*sem):
    return pltpu.CompilerParams(dimension_semantics=sem, vmem_limit_bytes=VMEM_LIMIT)


def _rms(x, g):
    return x * lax.rsqrt(jnp.mean(x * x, axis=-1, keepdims=True) + RMS_EPS) * g


def _rows(table, idx):
    return table.at[idx].get(mode="promise_in_bounds")


ORDER_SPEC = pl.BlockSpec(memory_space=pl.ANY)


def _ordered(kernel_fn, n_in):
    def ordered_kernel(*refs):
        return kernel_fn(*refs[:n_in], *refs[n_in + 1:])
    return ordered_kernel


def _rmsnorm_kernel(x_ref, g_ref, h_ref):
    h_ref[...] = _rms(x_ref[...], g_ref[...]).astype(h_ref.dtype)


def _rmsnorm(x, g, row0, n_rows):
    d = x.shape[1]
    off = row0 // TM_ROW
    return pl.pallas_call(
        _rmsnorm_kernel,
        grid=(n_rows // TM_ROW,),
        in_specs=[pl.BlockSpec((TM_ROW, d), lambda i: (i + off, 0)),
                  pl.BlockSpec((1, d), lambda i: (0, 0), pipeline_mode=CONST_BLOCK)],
        out_specs=pl.BlockSpec((TM_ROW, d), lambda i: (i, 0)),
        out_shape=jax.ShapeDtypeStruct((n_rows, d), BF16),
        compiler_params=_params("parallel"),
        name="rmsnorm",
    )(x, g.reshape(1, d))


def _proj_kernel(h_ref, w_ref, cg_ref, bd_ref, o_ref, *, n_norm_tiles, n_col_tiles):
    j = pl.program_id(1)
    tm, tn = h_ref.shape[0], w_ref.shape[1]

    def tile(normed):
        for r in range(tm // SUB_ROWS):
            rows = pl.ds(r * SUB_ROWS, SUB_ROWS)
            acc = jnp.dot(h_ref[rows, :], w_ref[...], preferred_element_type=F32)
            if normed:
                sq = (acc * acc).astype(BF16)
                ms = jnp.concatenate(
                    [jnp.dot(sq[:, c * MXU_TILE:(c + 1) * MXU_TILE], bd_ref[...], preferred_element_type=F32)
                     for c in range(tn // MXU_TILE)], axis=1)
                acc = acc * lax.rsqrt(ms + RMS_EPS) * cg_ref[...]
            o_ref[0, rows, :] = acc.astype(o_ref.dtype)

    if n_norm_tiles > 0:
        pl.when(j < n_norm_tiles)(lambda: tile(True))
    if n_norm_tiles < n_col_tiles:
        pl.when(j >= n_norm_tiles)(lambda: tile(False))


def _proj(h, w, col_gain, n_norm_tiles, after):
    t, d = h.shape
    n = w.shape[1]
    tn = HEAD_TILE
    bd = jnp.asarray(np.kron(np.eye(MXU_TILE // HEAD_DIM), np.full((HEAD_DIM, HEAD_DIM), 1.0 / HEAD_DIM)), BF16)
    return pl.pallas_call(
        _ordered(functools.partial(_proj_kernel, n_norm_tiles=n_norm_tiles, n_col_tiles=n // tn), 4),
        grid=(t // TM_PROJ, n // tn),
        in_specs=[
            pl.BlockSpec((TM_PROJ, d), lambda i, j: (i, 0)),
            pl.BlockSpec((d, tn), lambda i, j: (0, j)),
            pl.BlockSpec((1, tn), lambda i, j: (0, j)),
            pl.BlockSpec((MXU_TILE, MXU_TILE), lambda i, j: (0, 0), pipeline_mode=CONST_BLOCK),
            ORDER_SPEC,
        ],
        out_specs=pl.BlockSpec((1, TM_PROJ, tn), lambda i, j: (j, i, 0)),
        out_shape=jax.ShapeDtypeStruct((n // tn, t, tn), BF16),
        compiler_params=_params("parallel", "arbitrary"),
        name="proj",
    )(h, w, col_gain.reshape(1, n), bd, after)


def _attn_kernel(q_ref, kin_ref, vin_ref, bias_ref, o_ref, k_scr, v_scr, *, n_prev, has_sink):
    c = pl.program_id(1)
    band = (n_prev + 1) * CHUNK
    pad = n_prev * CHUNK
    width = bias_ref.shape[-1]
    lo = lax.broadcasted_iota(jnp.int32, (CHUNK, LANES), 1) < HEAD_DIM

    @pl.when(c == 0)
    def _():
        lo_s = lax.broadcasted_iota(jnp.int32, (SEQ, LANES), 1) < HEAD_DIM
        for src, dst in ((kin_ref, k_scr), (vin_ref, v_scr)):
            dst[:, :pad, :] = jnp.zeros((N_KV_HEADS, pad, LANES), dst.dtype)
            dst[:, pad + SEQ:, :] = jnp.zeros((N_KV_HEADS, CHUNK, LANES), dst.dtype)
            for hp in range(N_KV_HEADS // 2):
                two = src[0, :, hp * LANES:(hp + 1) * LANES].astype(F32)
                swapped = pltpu.roll(two, HEAD_DIM, axis=1)
                dst[2 * hp, pad:pad + SEQ, :] = jnp.where(lo_s, two, swapped).astype(dst.dtype)
                dst[2 * hp + 1, pad:pad + SEQ, :] = jnp.where(lo_s, swapped, two).astype(dst.dtype)

    row0 = pl.multiple_of(c * CHUNK, CHUNK)
    kslot = lax.broadcasted_iota(jnp.int32, (1, width), 1)
    first_valid = (n_prev - c) * CHUNK
    keep = (kslot >= first_valid) & (kslot < band)
    pairs = KV_GROUP // 2
    def one_kv_head(h):
        qh = q_ref[h]
        zero = jnp.zeros((CHUNK, LANES), qh.dtype)
        parts = []
        for gg in range(pairs):
            qp = qh[:, gg * LANES:(gg + 1) * LANES]
            parts.append(jnp.where(lo, qp, zero))
            parts.append(jnp.where(lo, zero, qp))
        lhs = jnp.concatenate(parts, axis=0)
        k2 = k_scr[h, pl.ds(row0, width), :]
        v2 = v_scr[h, pl.ds(row0, width), :]
        s = lax.dot_general(lhs, k2, (((1,), (1,)), ((), ())), preferred_element_type=F32)
        bias = bias_ref[h * KV_GROUP:(h + 1) * KV_GROUP].reshape(KV_GROUP * CHUNK, width)
        if has_sink:
            s = jnp.where(keep, s, 0.0) + jnp.where(kslot < first_valid, NEG_INF, bias)
        else:
            s = jnp.where(keep, s + bias, NEG_INF)
        m = jnp.max(s, axis=-1, keepdims=True)
        p = jnp.exp2(s - m)
        l = jnp.sum(p, axis=-1, keepdims=True)
        if has_sink:
            tail = jnp.where(kslot[:, width - LANES:] < band, p[:, width - LANES:], 0.0)
            p = jnp.concatenate([p[:, :width - LANES], tail], axis=1)
        o2 = jnp.dot(p.astype(BF16), v2, preferred_element_type=F32) * (1.0 / l)
        outs = []
        for gg in range(pairs):
            a = o2[(2 * gg) * CHUNK:(2 * gg + 1) * CHUNK]
            b = o2[(2 * gg + 1) * CHUNK:(2 * gg + 2) * CHUNK]
            outs.append(jnp.where(lo, a, b))
        o_ref[h] = jnp.concatenate(outs, axis=1).astype(o_ref.dtype)

    for h in range(N_KV_HEADS):
        one_kv_head(h)


def _attention(q3, kv3, k_tile, v_tile, bias_ext, n_prev, has_sink, after):
    t = q3.shape[1]
    nb = t // SEQ
    nc = SEQ // CHUNK
    width = bias_ext.shape[-1]
    spad = n_prev * CHUNK + SEQ + CHUNK
    return pl.pallas_call(
        _ordered(functools.partial(_attn_kernel, n_prev=n_prev, has_sink=has_sink), 4),
        grid=(nb, nc),
        in_specs=[
            pl.BlockSpec((N_KV_HEADS, CHUNK, HEAD_TILE), lambda b, c: (0, b * nc + c, 0)),
            pl.BlockSpec((1, SEQ, KV_WIDTH), lambda b, c: (k_tile, b, 0)),
            pl.BlockSpec((1, SEQ, KV_WIDTH), lambda b, c: (v_tile, b, 0)),
            pl.BlockSpec((N_HEADS, CHUNK, width), lambda b, c: (0, 0, 0), pipeline_mode=CONST_BLOCK),
            ORDER_SPEC,
        ],
        out_specs=pl.BlockSpec((N_KV_HEADS, CHUNK, HEAD_TILE), lambda b, c: (0, b * nc + c, 0)),
        out_shape=jax.ShapeDtypeStruct((N_KV_HEADS, t, HEAD_TILE), BF16),
        scratch_shapes=[pltpu.VMEM((N_KV_HEADS, spad, LANES), BF16), pltpu.VMEM((N_KV_HEADS, spad, LANES), BF16)],
        compiler_params=_params("parallel", "arbitrary"),
        name="band_attention",
    )(q3, kv3, kv3, bias_ext, after)


def _bias_ext(bias, sinks):
    fill = jnp.full((N_HEADS, CHUNK, CHUNK), NEG_INF, F32)
    if sinks is not None:
        col = lax.broadcasted_iota(jnp.int32, fill.shape, 2)
        fill = jnp.where(col == 0, sinks.astype(F32)[:, None, None] * LOG2E, fill)
    return jnp.concatenate([bias.astype(F32) * LOG2E, fill], axis=-1)


def _out_proj_kernel(a_ref, w_ref, r_ref, o_ref):
    acc = r_ref[...]
    for h in range(N_KV_HEADS):
        acc = acc + jnp.dot(a_ref[h], w_ref[h], preferred_element_type=F32)
    o_ref[...] = acc


def _out_proj(a3, w, res, row0, after):
    t = a3.shape[1]
    d = res.shape[1]
    tn = HEAD_TILE
    off = row0 // TM_PROJ
    w3 = w.reshape(N_KV_HEADS, HEAD_TILE, d)
    return pl.pallas_call(
        _ordered(_out_proj_kernel, 3),
        grid=(t // TM_PROJ, d // tn),
        in_specs=[
            pl.BlockSpec((N_KV_HEADS, TM_PROJ, HEAD_TILE), lambda i, j: (0, i, 0)),
            pl.BlockSpec((N_KV_HEADS, HEAD_TILE, tn), lambda i, j: (0, 0, j)),
            pl.BlockSpec((TM_PROJ, tn), lambda i, j: (i + off, j)),
            ORDER_SPEC,
        ],
        out_specs=pl.BlockSpec((TM_PROJ, tn), lambda i, j: (i, j)),
        out_shape=jax.ShapeDtypeStruct((t, d), F32),
        compiler_params=_params("parallel", "arbitrary"),
        name="out_proj",
    )(a3, w3, res, after)


def _router_kernel(x_ref, g_ref, w_ref, h_ref, route_ref):
    hf = _rms(x_ref[...], g_ref[...])
    hb = hf.astype(BF16)
    h_ref[...] = hb
    hlo = (hf - hb.astype(F32)).astype(BF16)
    both = jnp.dot(hb, w_ref[...], preferred_element_type=F32)
    logits = (both[:, :ROUTER_LANES] + both[:, ROUTER_LANES:]
              + jnp.dot(hlo, w_ref[:, :ROUTER_LANES], preferred_element_type=F32))
    lane = lax.broadcasted_iota(jnp.int32, logits.shape, 1).astype(F32)
    big = float(ROUTER_LANES)
    is_g = lane < N_GROUPS
    gl = jnp.where(is_g, logits, NEG_INF)
    gmax = jnp.max(gl, axis=-1, keepdims=True)
    gidx = jnp.min(jnp.where(gl == gmax, lane, big), axis=-1, keepdims=True)
    gsum = jnp.sum(jnp.where(is_g, jnp.exp(gl - gmax), 0.0), axis=-1, keepdims=True)
    g_w = 1.0 / gsum
    e0 = N_GROUPS + gidx * N_EXPERTS_PER_GROUP
    el = jnp.where((lane >= e0) & (lane < e0 + N_EXPERTS_PER_GROUP), logits, NEG_INF)
    m1 = jnp.max(el, axis=-1, keepdims=True)
    i1 = jnp.min(jnp.where(el == m1, lane, big), axis=-1, keepdims=True)
    el2 = jnp.where(lane == i1, NEG_INF, el)
    m2 = jnp.max(el2, axis=-1, keepdims=True)
    i2 = jnp.min(jnp.where(el2 == m2, lane, big), axis=-1, keepdims=True)
    ex = jnp.exp(m2 - m1)
    w1 = 1.0 / (1.0 + ex)
    w2 = ex * w1
    route = jnp.where(lane == 0, i1 - N_GROUPS,
                      jnp.where(lane == 1, i2 - N_GROUPS,
                                jnp.where(lane == 2, g_w * w1,
                                          jnp.where(lane == 3, g_w * w2, 0.0))))
    route_ref[...] = route


def _router(x, g, w_hi_lo, after):
    t, d = x.shape
    return pl.pallas_call(
        _ordered(_router_kernel, 3),
        grid=(t // TM_ROW,),
        in_specs=[
            pl.BlockSpec((TM_ROW, d), lambda i: (i, 0)),
            pl.BlockSpec((1, d), lambda i: (0, 0), pipeline_mode=CONST_BLOCK),
            pl.BlockSpec((d, 2 * ROUTER_LANES), lambda i: (0, 0), pipeline_mode=CONST_BLOCK),
            ORDER_SPEC,
        ],
        out_specs=[
            pl.BlockSpec((TM_ROW, d), lambda i: (i, 0)),
            pl.BlockSpec((TM_ROW, ROUTER_LANES), lambda i: (i, 0)),
        ],
        out_shape=[jax.ShapeDtypeStruct((t, d), BF16), jax.ShapeDtypeStruct((t, ROUTER_LANES), F32)],
        compiler_params=_params("parallel"),
        name="router",
    )(x, g.reshape(1, d), w_hi_lo, after)


def _router_weights(w_rg, w_re):
    d = w_rg.shape[0]
    w = jnp.concatenate([w_rg, w_re.transpose(1, 0, 2).reshape(d, N_EXPERTS)], axis=1)
    w = jnp.pad(w, ((0, 0), (0, ROUTER_LANES - w.shape[1])))
    whi = w.astype(BF16)
    return jnp.concatenate([whi, (w - whi.astype(F32)).astype(BF16)], axis=1)


def _experts_kernel(te_ref, nu_ref, xs_ref, cw_ref, wg_ref, wu_ref, wd_ref, y_ref):
    i = pl.program_id(0)

    @pl.when(i < nu_ref[0])
    def _():
        xs = xs_ref[...]
        a = jnp.dot(xs, wg_ref[0], preferred_element_type=F32)
        u = jnp.dot(xs, wu_ref[0], preferred_element_type=F32)
        act = a * jax.nn.sigmoid(a) * u * cw_ref[...]
        y_ref[...] = jnp.dot(act.astype(BF16), wd_ref[0], preferred_element_type=F32).astype(y_ref.dtype)

    @pl.when(i >= nu_ref[0])
    def _():
        y_ref[...] = jnp.zeros_like(y_ref)


def _experts(xs, cw, tile_expert, n_used, wg, wu, wd, layer, after):
    n_rows, d = xs.shape
    f = wg.shape[2]
    w_idx = lambda i, te, nu: (te[i] + layer * N_EXPERTS, 0, 0)
    used_idx = lambda i, te, nu: (jnp.minimum(i, nu[0] - 1), 0)
    grid_spec = pltpu.PrefetchScalarGridSpec(
        num_scalar_prefetch=2,
        grid=(n_rows // TM_EXPERT,),
        in_specs=[
            pl.BlockSpec((TM_EXPERT, d), used_idx),
            pl.BlockSpec((TM_EXPERT, 1), used_idx),
            pl.BlockSpec((1, d, f), w_idx),
            pl.BlockSpec((1, d, f), w_idx),
            pl.BlockSpec((1, f, d), w_idx),
            ORDER_SPEC,
        ],
        out_specs=pl.BlockSpec((TM_EXPERT, d), lambda i, te, nu: (i, 0)),
    )
    return pl.pallas_call(
        _ordered(_experts_kernel, 7),
        grid_spec=grid_spec,
        out_shape=jax.ShapeDtypeStruct((n_rows, d), BF16),
        compiler_params=_params("arbitrary"),
        name="experts",
    )(tile_expert, n_used, xs, cw, wg, wu, wd, after)


def _dispatch_plan(route):
    t = route.shape[0]
    n_assign = t * TOP_K
    n_fill = N_EXPERTS * TM_EXPERT
    n_rows = n_assign + n_fill
    ef = route[:, :TOP_K].T.reshape(-1).astype(jnp.int32)
    wf = route[:, TOP_K:2 * TOP_K].T.reshape(-1)
    iota = jnp.arange(n_assign, dtype=jnp.int32)
    se, order, ws = lax.sort((ef, iota, wf), num_keys=1, is_stable=True)
    edges = jnp.arange(1, N_EXPERTS + 1, dtype=jnp.int32)
    ends = jnp.sum((se[None, :] < edges[:, None]).astype(jnp.int32), axis=1)
    starts = jnp.concatenate([jnp.zeros((1,), jnp.int32), ends[:-1]])
    counts = ends - starts
    padded = (counts + TM_EXPERT - 1) // TM_EXPERT * TM_EXPERT
    pend = jnp.cumsum(padded)
    pstart = pend - padded
    shift = pstart - starts
    dshift = shift - jnp.concatenate([jnp.zeros((1,), jnp.int32), shift[:-1]])
    dest = iota + jnp.sum(jnp.where(iota[None, :] >= starts[:, None], dshift[:, None], 0), axis=0)
    jj = jnp.arange(TM_EXPERT, dtype=jnp.int32)[None, :]
    fill_key = jnp.where(jj < (padded - counts)[:, None], (pstart + counts)[:, None] + jj, n_rows).reshape(-1)
    keys = jnp.concatenate([dest, fill_key])
    toks = jnp.concatenate([order % t, jnp.arange(n_fill, dtype=jnp.int32) % t])
    wts = jnp.concatenate([ws, jnp.zeros((n_fill,), F32)])
    _, src_token, cw = lax.sort((keys, toks, wts), num_keys=1)
    _, pos = lax.sort((order, dest), num_keys=1)
    n_used = (pend[-1] // TM_EXPERT).reshape(1)
    tile_ids = jnp.arange(n_rows // TM_EXPERT, dtype=jnp.int32)
    te = jnp.sum((pend[None, :] // TM_EXPERT <= jnp.minimum(tile_ids, n_used - 1)[:, None]).astype(jnp.int32), axis=1)
    te = jnp.minimum(te, N_EXPERTS - 1)
    return src_token, cw.reshape(n_rows, 1), te, n_used, pos


def _ple_kernel(x_ref, yy_ref, p_ref, g_ref, wgd_ref, wgu_ref, wp_ref, ng_ref, o_ref, *h_refs):
    x = x_ref[...] + yy_ref[0].astype(F32) + yy_ref[1].astype(F32)
    hn = _rms(x, g_ref[...]).astype(BF16)
    tdown = jnp.dot(hn, wgd_ref[...], preferred_element_type=F32)
    gate = jax.nn.sigmoid(jnp.dot(tdown.astype(BF16), wgu_ref[...], preferred_element_type=F32))
    pp = jnp.dot(p_ref[...].astype(BF16), wp_ref[...], preferred_element_type=F32)
    out = x + gate * pp
    o_ref[...] = out
    if h_refs:
        normed = out * lax.rsqrt(jnp.mean(out * out, axis=-1, keepdims=True) + RMS_EPS)
        for k, h_ref in enumerate(h_refs):
            h_ref[...] = (normed * ng_ref[k:k + 1, :]).astype(h_ref.dtype)


def _ple(x, yy, p, p_row0, g, wgd, wgu, wp, next_gains, after):
    t, d = x.shape
    off = p_row0 // TM_PLE
    n_next = len(next_gains)
    ng = jnp.stack(next_gains) if n_next else jnp.ones((1, d), F32)
    row_spec = pl.BlockSpec((TM_PLE, d), lambda i: (i, 0))
    return pl.pallas_call(
        _ordered(_ple_kernel, 8),
        grid=(t // TM_PLE,),
        in_specs=[
            row_spec,
            pl.BlockSpec((TOP_K, TM_PLE, d), lambda i: (0, i, 0)),
            pl.BlockSpec((TM_PLE, PLE_DIM), lambda i: (i + off, 0)),
            pl.BlockSpec((1, d), lambda i: (0, 0), pipeline_mode=CONST_BLOCK),
            pl.BlockSpec((d, PLE_DIM), lambda i: (0, 0), pipeline_mode=CONST_BLOCK),
            pl.BlockSpec((PLE_DIM, d), lambda i: (0, 0), pipeline_mode=CONST_BLOCK),
            pl.BlockSpec((PLE_DIM, d), lambda i: (0, 0), pipeline_mode=CONST_BLOCK),
            pl.BlockSpec(ng.shape, lambda i: (0, 0), pipeline_mode=CONST_BLOCK),
            ORDER_SPEC,
        ],
        out_specs=[row_spec] * (1 + n_next),
        out_shape=[jax.ShapeDtypeStruct((t, d), F32)] + [jax.ShapeDtypeStruct((t, d), BF16)] * n_next,
        compiler_params=_params("parallel"),
        name="ple",
    )(x, yy, p, g.reshape(1, d), wgd, wgu, wp, ng, after)


def _t5_bucket(rel):
    nb = N_BUCKETS // 2
    n = -rel
    ret = np.where(n < 0, nb, 0)
    n = np.abs(n)
    max_exact = nb // 2
    large = max_exact + (np.log(np.maximum(n, 1) / max_exact)
                         / np.log(T5_MAX_DIST / max_exact) * (nb - max_exact)).astype(np.int32)
    large = np.minimum(large, nb - 1)
    return (ret + np.where(n < max_exact, n, large)).astype(np.int32)


def _clipped_distance(rel):
    return np.clip(-rel, -REL_CLIP, REL_CLIP) + REL_CLIP


def _band_bias(table, rel_to_index, n_prev):
    band = (n_prev + 1) * CHUNK
    rel = np.arange(band + CHUNK - 1) - (CHUNK - 1) - n_prev * CHUNK
    ext = table.astype(F32)[:, rel_to_index(rel)]
    return jnp.stack([ext[:, CHUNK - 1 - q:CHUNK - 1 - q + band] for q in range(CHUNK)], axis=1)


def _head_gain(g, n_heads, scale=1.0):
    return jnp.tile(g.astype(F32) * scale, n_heads)


def kernel(x, p, t5_bias, attn_norm_a, w_qkv_a, q_norm_a, k_norm_a, sinks_a, w_o_a, kv_norm_b, w_kv_b, k_norm_b, attn_norm_b, w_q_b, q_norm_b, rel_bias_b, w_o_b, ffn_norm, w_router_group, w_router_expert, w_exp_gate, w_exp_up, w_exp_down, w_ple_proj, ple_norm, w_ple_gate_down, w_ple_gate_up):
    b, s, d = x.shape
    t = b * s
    ts = t // N_STREAMS
    f = w_exp_gate.shape[-1]
    x_in = x.reshape(t, d)
    p_flat = p.reshape(DEPTH * t, PLE_DIM)
    q_scale = HEAD_DIM ** -0.5 * LOG2E
    bias_a = _band_bias(t5_bias, _t5_bucket, WIN_CHUNKS)
    ones_kv = jnp.ones((KV_WIDTH,), F32)
    wg_all = w_exp_gate.astype(BF16).reshape(DEPTH * N_EXPERTS, d, f)
    wu_all = w_exp_up.astype(BF16).reshape(DEPTH * N_EXPERTS, d, f)
    wd_all = w_exp_down.astype(BF16).reshape(DEPTH * N_EXPERTS, f, d)
    streams = range(N_STREAMS)
    xs_ = [None] * N_STREAMS
    hs_ = [_rmsnorm(x_in, attn_norm_a[0], k * ts, ts) for k in streams]
    kv_b = [None] * N_STREAMS
    last = hs_[-1]

    def proj_weights(i):
        if i < N_A_LAYERS:
            gains = jnp.concatenate([_head_gain(q_norm_a[i], N_HEADS, q_scale),
                                     _head_gain(k_norm_a[i], N_KV_HEADS), ones_kv])
            return w_qkv_a[i].astype(BF16), gains, N_KV_HEADS + 1
        j = i - N_A_LAYERS
        return w_q_b[j].astype(BF16), _head_gain(q_norm_b[j], N_HEADS, q_scale), N_KV_HEADS

    lead, trail = 0, 1
    q3 = [None] * N_STREAMS
    q3[lead] = last = _proj(hs_[lead], *proj_weights(0), last)
    for i in range(DEPTH):
        is_a = i < N_A_LAYERS
        j = i - N_A_LAYERS
        if is_a:
            bias_ext = _bias_ext(bias_a, sinks_a[i])
            w_out = w_o_a[i].astype(BF16)
        else:
            bias_ext = _bias_ext(_band_bias(rel_bias_b[j], _clipped_distance, B_PREV_CHUNKS), None)
            w_out = w_o_b[j].astype(BF16)
        w_router = _router_weights(w_router_group[i], w_router_expert[i])
        wgd, wgu, wpp = (w_ple_gate_down[i].astype(BF16), w_ple_gate_up[i].astype(BF16),
                         w_ple_proj[i].astype(BF16))
        if i + 1 < DEPTH:
            next_gains = [attn_norm_a[i + 1] if i + 1 < N_A_LAYERS else attn_norm_b[i + 1 - N_A_LAYERS]]
            if i == N_A_LAYERS - 1:
                next_gains.append(kv_norm_b)
        else:
            next_gains = []

        def attend(k, after):
            if is_a:
                return _attention(q3[k], q3[k], N_KV_HEADS, N_KV_HEADS + 1, bias_ext, WIN_CHUNKS, True, after)
            return _attention(q3[k], kv_b[k], 0, 1, bias_ext, B_PREV_CHUNKS, False, after)

        def route(k, o3, after):
            res, row0 = (x_in, k * ts) if i == 0 else (xs_[k], 0)
            xs_[k] = _out_proj(o3, w_out, res, row0, after)
            hk, rt = _router(xs_[k], ffn_norm[i], w_router, xs_[k])
            return (hk,) + _dispatch_plan(rt)

        def experts(k, routed, after):
            hk, src_token, cw, tile_expert, n_used, pos = routed
            y = _experts(_rows(hk, src_token), cw, tile_expert, n_used, wg_all, wu_all, wd_all, i, after)
            return y, pos

        def finish(k, y, pos, after):
            yy = _rows(y, pos).reshape(TOP_K, ts, d)
            outs = _ple(xs_[k], yy, p_flat, i * t + k * ts, ple_norm[i], wgd, wgu, wpp, next_gains, after)
            xs_[k] = tail = outs[0]
            if next_gains:
                hs_[k] = outs[1]
            if i == N_A_LAYERS - 1:
                kv_gain = jnp.concatenate([_head_gain(k_norm_b, N_KV_HEADS), ones_kv])
                kv_b[k] = tail = _proj(outs[2], w_kv_b.astype(BF16), kv_gain, 1, tail)
            return tail

        o3_lead = last = attend(lead, last)
        q3[trail] = last = _proj(hs_[trail], *proj_weights(i), last)
        routed_lead = route(lead, o3_lead, last)
        o3_trail = last = attend(trail, routed_lead[1])
        routed_trail = route(trail, o3_trail, last)
        y_lead, pos_lead = experts(lead, routed_lead, routed_trail[1])
        y_trail, pos_trail = experts(trail, routed_trail, y_lead)
        last = finish(lead, y_lead, pos_lead, y_trail)
        if i + 1 < DEPTH:
            q3[lead] = last = _proj(hs_[lead], *proj_weights(i + 1), last)
        last = finish(trail, y_trail, pos_trail, last)
    return jnp.concatenate(xs_, axis=0).reshape(b, s, d)
```
